```python
import math
import jax, jax.numpy as jnp
from jax import lax
import numpy as np


D_MODEL = 1024
BATCH = 16
SEQ = 2048
DEPTH = 2

CHUNK = 64
HEAD_DIM = 64
ROPE_THETA = 10000.0
EPS = 1e-6
D_FF = 2816
Q_BLOCK = 128
DSA_Q_BLOCK = 32
A_HEADS = D_MODEL // (4 * HEAD_DIM)
A_VDIM = 2 * HEAD_DIM
B_HEADS = D_MODEL // (2 * HEAD_DIM)
IDX_HEADS = 8
IDX_DIM = 32
TOPK_MAX = 256
C_HEADS = D_MODEL // HEAD_DIM
C_LEFT_CHUNKS = 8
MAX_REL = 256
REL_TABLE = MAX_REL + CHUNK

MIX_WIDTH = A_HEADS * A_VDIM + B_HEADS * HEAD_DIM
EVEN_SIZES = (2 * A_HEADS * HEAD_DIM, 2 * A_HEADS * HEAD_DIM, A_HEADS * A_VDIM,
              B_HEADS * HEAD_DIM, B_HEADS * HEAD_DIM, B_HEADS * HEAD_DIM,
              IDX_HEADS * IDX_DIM, IDX_DIM, IDX_HEADS)
EVEN_PROJ = sum(EVEN_SIZES)
EVEN_SPLITS = [sum(EVEN_SIZES[:i + 1]) for i in range(len(EVEN_SIZES) - 1)]

kernel_name = "hybrid_chunk_causal_diff_dsa_band"


def rmsnorm(x, g):
    xf = x.astype(jnp.float32)
    y = xf * lax.rsqrt(jnp.mean(xf * xf, axis=-1, keepdims=True) + EPS)
    return (y * g.astype(jnp.float32)).astype(x.dtype)


def rope_tables(seq, dim):
    inv = ROPE_THETA ** (-jnp.arange(0, dim, 2, dtype=jnp.float32) / dim)
    ang = jnp.arange(seq, dtype=jnp.float32)[:, None] * inv[None, :]
    return jnp.cos(ang), jnp.sin(ang)


def apply_rope(x, cos, sin):
    xf = x.astype(jnp.float32)
    x1, x2 = jnp.split(xf, 2, axis=-1)
    c = cos[:, None, :]
    s = sin[:, None, :]
    return jnp.concatenate([x1 * c - x2 * s, x2 * c + x1 * s], axis=-1).astype(x.dtype)


def masked_softmax(s, mask):
    return jax.nn.softmax(jnp.where(mask, s, -jnp.inf), axis=-1)


def swiglu(h, wg, wu, wd):
    return (jax.nn.silu(h @ wg) * (h @ wu)) @ wd


def diff_attention(q, k, v, lam, subln, lambda_init):
    B_, S_ = q.shape[0], q.shape[1]
    nb = S_ // Q_BLOCK
    qb = q.reshape(B_, nb, Q_BLOCK, A_HEADS, 2, HEAD_DIM).swapaxes(0, 1)
    kf = k.astype(jnp.float32)
    vf = v.astype(jnp.float32)
    kchunk = jnp.arange(S_) // CHUNK
    scale = HEAD_DIM ** -0.5

    def one_block(args):
        bi, qblk = args
        qpos = bi * Q_BLOCK + jnp.arange(Q_BLOCK)
        mask = kchunk[None, :] <= (qpos // CHUNK)[:, None]
        s = jnp.einsum('bqhcd,bkhcd->bhcqk', qblk.astype(jnp.float32), kf) * scale
        p = masked_softmax(s, mask)
        w = p[:, :, 0] - lam * p[:, :, 1]
        return jnp.einsum('bhqk,bkhe->bqhe', w, vf)

    o = lax.map(one_block, (jnp.arange(nb), qb))
    o = o.swapaxes(0, 1).reshape(B_, S_, A_HEADS, A_VDIM)
    o = rmsnorm(o, subln) * (1.0 - lambda_init)
    return o.reshape(B_, S_, A_HEADS * A_VDIM).astype(v.dtype)


def dsa_attention(q, k, v, qi, ki, wi, topk):
    B_, S_ = q.shape[0], q.shape[1]
    nb = S_ // DSA_Q_BLOCK
    qb = q.reshape(B_, nb, DSA_Q_BLOCK, B_HEADS, HEAD_DIM).swapaxes(0, 1)
    qib = qi.reshape(B_, nb, DSA_Q_BLOCK, IDX_HEADS, IDX_DIM).swapaxes(0, 1)
    wib = wi.reshape(B_, nb, DSA_Q_BLOCK, IDX_HEADS).swapaxes(0, 1)
    kf = k.astype(jnp.float32)
    vf = v.astype(jnp.float32)
    kif = ki.astype(jnp.float32)
    kchunk = jnp.arange(S_) // CHUNK
    scale = HEAD_DIM ** -0.5
    gather = jax.vmap(lambda kk, ii: kk[ii])

    def one_block(args):
        bi, qblk, qiblk, wiblk = args
        qpos = bi * DSA_Q_BLOCK + jnp.arange(DSA_Q_BLOCK)
        qchunk = qpos // CHUNK
        rel = jax.nn.relu(jnp.einsum('bqhd,bkd->bqhk', qiblk.astype(jnp.float32), kif))
        score = jnp.einsum('bqhk,bqh->bqk', rel, wiblk.astype(jnp.float32))
        mask = kchunk[None, :] <= qchunk[:, None]
        score = jnp.where(mask[None], score, -jnp.inf)
        _, idx = lax.top_k(score, topk)
        valid = (idx // CHUNK) <= qchunk[None, :, None]
        kg = gather(kf, idx)
        vg = gather(vf, idx)
        s = jnp.einsum('bqhd,bqkhd->bhqk', qblk.astype(jnp.float32), kg) * scale
        p = masked_softmax(s, valid[:, None])
        return jnp.einsum('bhqk,bqkhd->bqhd', p, vg)

    o = lax.map(one_block, (jnp.arange(nb), qb, qib, wib))
    return o.swapaxes(0, 1).reshape(B_, S_, B_HEADS * HEAD_DIM).astype(v.dtype)


def chunk_band_attention(q, k, v, rel_bias):
    B_, S_ = q.shape[0], q.shape[1]
    nc = S_ // CHUNK
    left = C_LEFT_CHUNKS * CHUNK
    band = left + CHUNK
    kp = jnp.pad(k.astype(jnp.float32), ((0, 0), (left, 0), (0, 0), (0, 0)))
    vp = jnp.pad(v.astype(jnp.float32), ((0, 0), (left, 0), (0, 0), (0, 0)))
    dist = left + jnp.arange(CHUNK)[:, None] - jnp.arange(band)[None, :]
    bias_idx = jnp.clip(dist, -(CHUNK - 1), MAX_REL) + (CHUNK - 1)
    bias = rel_bias.astype(jnp.float32)[:, bias_idx]
    qc = q.reshape(B_, nc, CHUNK, C_HEADS, HEAD_DIM).swapaxes(0, 1)
    scale = HEAD_DIM ** -0.5

    def one_chunk(args):
        ci, qblk = args
        kb = lax.dynamic_slice_in_dim(kp, ci * CHUNK, band, axis=1)
        vb = lax.dynamic_slice_in_dim(vp, ci * CHUNK, band, axis=1)
        valid = (ci * CHUNK - left + jnp.arange(band)) >= 0
        s = jnp.einsum('bqhd,bkhd->bhqk', qblk.astype(jnp.float32), kb) * scale + bias[None]
        p = masked_softmax(s, valid[None, None, None, :])
        return jnp.einsum('bhqk,bkhd->bqhd', p, vb)

    o = lax.map(one_chunk, (jnp.arange(nc), qc))
    return o.swapaxes(0, 1).reshape(B_, S_, C_HEADS * HEAD_DIM).astype(v.dtype)


def even_mixer(h, w_in, w_out, lam_p, subln, rope64, rope32, lambda_init, topk):
    B_, S_ = h.shape[0], h.shape[1]
    cos64, sin64 = rope64
    cos32, sin32 = rope32
    a_q, a_k, a_v, b_q, b_k, b_v, i_q, i_k, i_w = jnp.split(h @ w_in, EVEN_SPLITS, axis=-1)
    a_q = apply_rope(a_q.reshape(B_, S_, 2 * A_HEADS, HEAD_DIM), cos64, sin64).reshape(B_, S_, A_HEADS, 2, HEAD_DIM)
    a_k = apply_rope(a_k.reshape(B_, S_, 2 * A_HEADS, HEAD_DIM), cos64, sin64).reshape(B_, S_, A_HEADS, 2, HEAD_DIM)
    a_v = a_v.reshape(B_, S_, A_HEADS, A_VDIM)
    lp = lam_p.astype(jnp.float32)
    lam = jnp.exp(jnp.sum(lp[0] * lp[1])) - jnp.exp(jnp.sum(lp[2] * lp[3])) + lambda_init
    o_a = diff_attention(a_q, a_k, a_v, lam, subln, lambda_init)
    b_q = apply_rope(b_q.reshape(B_, S_, B_HEADS, HEAD_DIM), cos64, sin64)
    b_k = apply_rope(b_k.reshape(B_, S_, B_HEADS, HEAD_DIM), cos64, sin64)
    b_v = b_v.reshape(B_, S_, B_HEADS, HEAD_DIM)
    i_q = apply_rope(i_q.reshape(B_, S_, IDX_HEADS, IDX_DIM), cos32, sin32)
    i_k = apply_rope(i_k.reshape(B_, S_, 1, IDX_DIM), cos32, sin32)[:, :, 0]
    i_w = i_w * (IDX_HEADS ** -0.5 * IDX_DIM ** -0.5)
    o_b = dsa_attention(b_q, b_k, b_v, i_q, i_k, i_w, topk)
    return jnp.concatenate([o_a, o_b], axis=-1) @ w_out


def odd_mixer(h, w_in, w_out, rel_bias):
    B_, S_ = h.shape[0], h.shape[1]
    q, k, v = jnp.split(h @ w_in, 3, axis=-1)
    q = q.reshape(B_, S_, C_HEADS, HEAD_DIM)
    k = k.reshape(B_, S_, C_HEADS, HEAD_DIM)
    v = v.reshape(B_, S_, C_HEADS, HEAD_DIM)
    return chunk_band_attention(q, k, v, rel_bias) @ w_out


def setup_inputs(seed: int = 0) -> dict:
    key = jax.random.key(seed)
    ks = jax.random.split(key, 12)
    n_even = (DEPTH + 1) // 2
    n_odd = DEPTH // 2
    nrm = jax.random.normal
    f32 = jnp.float32
    return {
        "x": nrm(ks[0], (BATCH, SEQ, D_MODEL), f32),
        "norm_g": 1.0 + 0.01 * nrm(ks[1], (DEPTH, 6, D_MODEL), f32),
        "ffn_wg": nrm(ks[2], (DEPTH, 2, D_MODEL, D_FF), f32) * D_MODEL ** -0.5,
        "ffn_wu": nrm(ks[3], (DEPTH, 2, D_MODEL, D_FF), f32) * D_MODEL ** -0.5,
        "ffn_wd": nrm(ks[4], (DEPTH, 2, D_FF, D_MODEL), f32) * D_FF ** -0.5,
        "even_w_in": nrm(ks[5], (n_even, D_MODEL, EVEN_PROJ), f32) * D_MODEL ** -0.5,
        "even_w_out": nrm(ks[6], (n_even, MIX_WIDTH, D_MODEL), f32) * MIX_WIDTH ** -0.5,
        "even_lambda": 0.1 * nrm(ks[7], (n_even, 4, HEAD_DIM), f32),
        "even_subln": 1.0 + 0.01 * nrm(ks[8], (n_even, A_VDIM), f32),
        "odd_w_in": nrm(ks[9], (n_odd, D_MODEL, 3 * D_MODEL), f32) * D_MODEL ** -0.5,
        "odd_w_out": nrm(ks[10], (n_odd, D_MODEL, D_MODEL), f32) * D_MODEL ** -0.5,
        "odd_rel_bias": 0.1 * nrm(ks[11], (n_odd, C_HEADS, REL_TABLE), f32),
    }


def reference(x, norm_g, ffn_wg, ffn_wu, ffn_wd, even_w_in, even_w_out, even_lambda, even_subln,
              odd_w_in, odd_w_out, odd_rel_bias):
    S_ = x.shape[1]
    topk = min(TOPK_MAX, S_ // 4)
    rope64 = rope_tables(S_, HEAD_DIM)
    rope32 = rope_tables(S_, IDX_DIM)
    for l in range(DEPTH):
        g = norm_g[l]
        h = rmsnorm(x, g[0])
        x = x + 0.5 * rmsnorm(swiglu(h, ffn_wg[l, 0], ffn_wu[l, 0], ffn_wd[l, 0]), g[1])
        h = rmsnorm(x, g[2])
        if l % 2 == 0:
            e = l // 2
            lambda_init = 0.8 - 0.6 * math.exp(-0.3 * l)
            m = even_mixer(h, even_w_in[e], even_w_out[e], even_lambda[e], even_subln[e],
                           rope64, rope32, lambda_init, topk)
        else:
            o = l // 2
            m = odd_mixer(h, odd_w_in[o], odd_w_out[o], odd_rel_bias[o])
        x = x + rmsnorm(m.astype(x.dtype), g[3])
        h = rmsnorm(x, g[4])
        x = x + 0.5 * rmsnorm(swiglu(h, ffn_wg[l, 1], ffn_wu[l, 1], ffn_wd[l, 1]), g[5])
    return x
```

```python
import functools
import math

import jax
import jax.numpy as jnp
from jax import lax
from jax.experimental import pallas as pl
from jax.experimental.pallas import tpu as pltpu

D_MODEL = 1024
CHUNK = 64
HEAD_DIM = 64
ROPE_THETA = 10000.0
EPS = 1e-6
D_FF = 2816
A_HEADS = D_MODEL // (4 * HEAD_DIM)
A_VDIM = 2 * HEAD_DIM
B_HEADS = D_MODEL // (2 * HEAD_DIM)
IDX_HEADS = 8
IDX_DIM = 32
TOPK_MAX = 256
C_HEADS = D_MODEL // HEAD_DIM
C_LEFT_CHUNKS = 8
MAX_REL = 256
REL_TABLE = MAX_REL + CHUNK

LANES = 128
NEG = -1e30
VMEM_LIMIT = 56 * 1024 * 1024

FFN_TM = 512
FFN_FC = 256
PROJ_TM = 512
PROJ_CC = 512
DIFF_TQ = 256
DSA_TQ = 128
BAND_TQ = 256

BF16 = jnp.bfloat16
F32 = jnp.float32


def _dot(a, b):
    return jnp.dot(a, b, preferred_element_type=F32)


def _dot_nt(a, b):
    return lax.dot_general(a, b, (((1,), (1,)), ((), ())), preferred_element_type=F32)


def _rms(xf, g):
    ms = jnp.mean(xf * xf, axis=-1, keepdims=True)
    return xf * lax.rsqrt(ms + EPS) * g


def _params(*sem):
    return pltpu.CompilerParams(dimension_semantics=sem, vmem_limit_bytes=VMEM_LIMIT)


def _ffn_kernel(*refs, n_mix, n_chunks):
    x_ref = refs[0]
    mix_refs = refs[1:1 + n_mix]
    wout_refs = refs[1 + n_mix:1 + 2 * n_mix]
    g_ref, wg_ref, wu_ref, wd_ref, out_ref = refs[1 + 2 * n_mix:]

    x = x_ref[...]
    if n_mix:
        m = _dot(mix_refs[0][...], wout_refs[0][...])
        for a_ref, w_ref in zip(mix_refs[1:], wout_refs[1:]):
            m = m + _dot(a_ref[...], w_ref[...])
        x = x + _rms(m, g_ref[0:1, :])
    h = _rms(x, g_ref[1:2, :]).astype(BF16)
    acc = jnp.zeros(x.shape, F32)
    for c in range(n_chunks):
        a = _dot(h, wg_ref[c])
        b = _dot(h, wu_ref[c])
        t = (a / (1.0 + jnp.exp(-a))) * b
        acc = acc + _dot(t.astype(BF16), wd_ref[c])
    out_ref[...] = x + 0.5 * _rms(acc, g_ref[2:3, :])


def _ffn_call(x2d, mixes, wouts, g3, wg, wu, wd):
    n, d = x2d.shape
    nc = D_FF // FFN_FC
    n_mix = len(mixes)
    row = lambda i: (i, 0)
    const2 = lambda i: (0, 0)
    const3 = lambda i: (0, 0, 0)
    in_specs = [pl.BlockSpec((FFN_TM, d), row)]
    in_specs += [pl.BlockSpec((FFN_TM, m.shape[1]), row) for m in mixes]
    in_specs += [pl.BlockSpec(w.shape, const2) for w in wouts]
    in_specs += [
        pl.BlockSpec(g3.shape, const2),
        pl.BlockSpec((nc, d, FFN_FC), const3),
        pl.BlockSpec((nc, d, FFN_FC), const3),
        pl.BlockSpec((nc, FFN_FC, d), const3),
    ]
    return pl.pallas_call(
        functools.partial(_ffn_kernel, n_mix=n_mix, n_chunks=nc),
        grid=(n // FFN_TM,),
        in_specs=in_specs,
        out_specs=pl.BlockSpec((FFN_TM, d), row),
        out_shape=jax.ShapeDtypeStruct((n, d), F32),
        compiler_params=_params("parallel"),
        name="ffn_mix" if n_mix else "ffn",
    )(x2d, *mixes, *wouts, g3, wg, wu, wd)


def _rope(p, cos_t, sin_t, half):
    lane = lax.broadcasted_iota(jnp.int32, p.shape, 1)
    first = (lane & (2 * half - 1)) < half
    partner = jnp.where(first, pltpu.roll(p, LANES - half, 1), pltpu.roll(p, half, 1))
    return p * cos_t + partner * sin_t


def _proj_even_kernel(x_ref, g_ref, w_ref, tab_ref, qkv_ref, idx_ref, iw_ref):
    h = _rms(x_ref[...], g_ref[0:1, :]).astype(BF16)
    c64, s64, c32, s32 = tab_ref[0], tab_ref[1], tab_ref[2], tab_ref[3]
    q_scale = HEAD_DIM ** -0.5
    for c in range(6):
        p = _dot(h, w_ref[:, c * PROJ_CC:(c + 1) * PROJ_CC])
        for j in range(PROJ_CC // LANES):
            pj = p[:, j * LANES:(j + 1) * LANES]
            if c in (0, 1, 3, 4):
                pj = _rope(pj, c64, s64, HEAD_DIM // 2)
            if c in (0, 3):
                pj = pj * q_scale
            col = c * PROJ_CC + j * LANES
            qkv_ref[:, col:col + LANES] = pj.astype(BF16)
    p = _dot(h, w_ref[:, 6 * PROJ_CC:7 * PROJ_CC])
    for j in range(3):
        pj = _rope(p[:, j * LANES:(j + 1) * LANES], c32, s32, IDX_DIM // 2)
        idx_ref[:, j * LANES:(j + 1) * LANES] = pj.astype(BF16)
    iw_ref[...] = p[:, 3 * LANES:4 * LANES] * (IDX_HEADS ** -0.5 * IDX_DIM ** -0.5)


def _proj_even_call(x2d, g, w, tabs, seq):
    n, d = x2d.shape
    steps_per_seq = seq // PROJ_TM
    return pl.pallas_call(
        _proj_even_kernel,
        grid=(n // PROJ_TM,),
        in_specs=[
            pl.BlockSpec((PROJ_TM, d), lambda i: (i, 0)),
            pl.BlockSpec(g.shape, lambda i: (0, 0)),
            pl.BlockSpec(w.shape, lambda i: (0, 0)),
            pl.BlockSpec((4, PROJ_TM, LANES), lambda i: (0, i % steps_per_seq, 0)),
        ],
        out_specs=[
            pl.BlockSpec((PROJ_TM, 6 * PROJ_CC), lambda i: (i, 0)),
            pl.BlockSpec((PROJ_TM, 3 * LANES), lambda i: (i, 0)),
            pl.BlockSpec((PROJ_TM, LANES), lambda i: (i, 0)),
        ],
        out_shape=[
            jax.ShapeDtypeStruct((n, 6 * PROJ_CC), BF16),
            jax.ShapeDtypeStruct((n, 3 * LANES), BF16),
            jax.ShapeDtypeStruct((n, LANES), F32),
        ],
        compiler_params=_params("parallel"),
        name="proj_even",
    )(x2d, g, w, tabs)


def _proj_odd_kernel(x_ref, g_ref, w_ref, qkv_ref):
    h = _rms(x_ref[...], g_ref[0:1, :]).astype(BF16)
    q_scale = HEAD_DIM ** -0.5
    for c in range(3 * D_MODEL // PROJ_CC):
        p = _dot(h, w_ref[:, c * PROJ_CC:(c + 1) * PROJ_CC])
        if c * PROJ_CC < D_MODEL:
            p = p * q_scale
        qkv_ref[:, c * PROJ_CC:(c + 1) * PROJ_CC] = p.astype(BF16)


def _proj_odd_call(x2d, g, w):
    n, d = x2d.shape
    return pl.pallas_call(
        _proj_odd_kernel,
        grid=(n // PROJ_TM,),
        in_specs=[
            pl.BlockSpec((PROJ_TM, d), lambda i: (i, 0)),
            pl.BlockSpec(g.shape, lambda i: (0, 0)),
            pl.BlockSpec(w.shape, lambda i: (0, 0)),
        ],
        out_specs=pl.BlockSpec((PROJ_TM, 3 * d), lambda i: (i, 0)),
        out_shape=jax.ShapeDtypeStruct((n, 3 * d), BF16),
        compiler_params=_params("parallel"),
        name="proj_odd",
    )(x2d, g, w)


def _diff_attn_kernel(q_ref, k_ref, v_ref, lam_ref, sub_ref, o_ref, *, lambda_init):
    i = pl.program_id(2)
    tq = q_ref.shape[0]
    q = q_ref[...]
    lane = lax.broadcasted_iota(jnp.int32, q.shape, 1)
    zero = jnp.zeros_like(q)
    q0 = jnp.where(lane < HEAD_DIM, q, zero)
    q1 = jnp.where(lane >= HEAD_DIM, q, zero)
    rowc = (i * tq + lax.broadcasted_iota(jnp.int32, (tq, tq), 0)) >> 6
    colc0 = lax.broadcasted_iota(jnp.int32, (tq, tq), 1)

    def body(j, carry):
        start = pl.multiple_of(j * tq, tq)
        kj = k_ref[pl.ds(start, tq), :]
        vj = v_ref[pl.ds(start, tq), :]
        mask = ((j * tq + colc0) >> 6) <= rowc
        out = []
        for qc, (m, l, acc) in zip((q0, q1), (carry[:3], carry[3:])):
            s = jnp.where(mask, _dot_nt(qc, kj), NEG)
            m_new = jnp.maximum(m, jnp.max(s, axis=-1, keepdims=True))
            alpha = jnp.exp(m - m_new)
            p = jnp.exp(s - m_new)
            l = alpha * l + jnp.sum(p, axis=-1, keepdims=True)
            acc = alpha * acc + _dot(p.astype(BF16), vj)
            out += [m_new, l, acc]
        return tuple(out)

    m0 = jnp.full((tq, 1), NEG, F32)
    l0 = jnp.zeros((tq, 1), F32)
    a0 = jnp.zeros((tq, LANES), F32)
    _, l_0, acc_0, _, l_1, acc_1 = lax.fori_loop(0, i + 1, body, (m0, l0, a0, m0, l0, a0))

    lp = lam_ref[...]
    lam = (jnp.exp(jnp.sum(lp[0:1] * lp[1:2], axis=-1, keepdims=True))
           - jnp.exp(jnp.sum(lp[2:3] * lp[3:4], axis=-1, keepdims=True)) + lambda_init)
    o = acc_0 / l_0 - lam * (acc_1 / l_1)
    o = _rms(o, sub_ref[...]) * (1.0 - lambda_init)
    o_ref[...] = o.astype(o_ref.dtype)


def _diff_attn_call(qkv, lam_p, subln, batch, seq, lambda_init):
    n = qkv.shape[0]
    nq = seq // DIFF_TQ
    return pl.pallas_call(
        functools.partial(_diff_attn_kernel, lambda_init=lambda_init),
        grid=(batch, A_HEADS, nq),
        in_specs=[
            pl.BlockSpec((DIFF_TQ, LANES), lambda b, h, i: (b * nq + i, h)),
            pl.BlockSpec((seq, LANES), lambda b, h, i: (b, A_HEADS + h)),
            pl.BlockSpec((seq, LANES), lambda b, h, i: (b, 2 * A_HEADS + h)),
            pl.BlockSpec(lam_p.shape, lambda b, h, i: (0, 0)),
            pl.BlockSpec(subln.shape, lambda b, h, i: (0, 0)),
        ],
        out_specs=pl.BlockSpec((DIFF_TQ, LANES), lambda b, h, i: (b * nq + i, h)),
        out_shape=jax.ShapeDtypeStruct((n, A_HEADS * A_VDIM), BF16),
        compiler_params=_params("parallel", "parallel", "arbitrary"),
        name="diff_attn",
    )(qkv, qkv, qkv, lam_p, subln)


def _dsa_kernel(iq_ref, ik_ref, iw_ref, q_ref, k_ref, v_ref, o_ref, key_ref, bias_ref, *, topk):
    i = pl.program_id(1)
    tq = iq_ref.shape[0]
    seq = ik_ref.shape[0]
    lane = lax.broadcasted_iota(jnp.int32, (tq, LANES), 1)
    rowc = (i * tq + lax.broadcasted_iota(jnp.int32, (tq, seq), 0)) >> 6
    colc = lax.broadcasted_iota(jnp.int32, (tq, seq), 1) >> 6
    visible = colc <= rowc

    ik = ik_ref[...]
    iw = iw_ref[...]
    score = jnp.zeros((tq, seq), F32)
    heads_per_vreg = LANES // IDX_DIM
    for h in range(IDX_HEADS):
        blk = h // heads_per_vreg
        qh = iq_ref[:, blk * LANES:(blk + 1) * LANES]
        qh = jnp.where((lane >> 5) == (h % heads_per_vreg), qh, jnp.zeros_like(qh))
        rel = jnp.maximum(_dot_nt(qh, ik), 0.0)
        score = score + rel * iw[:, h:h + 1]
    score = jnp.where(visible, score + 0.0, -jnp.inf)

    bits = pltpu.bitcast(score, jnp.int32)
    key_ref[...] = bits ^ ((bits >> 31) & 0x7FFFFFFF)

    kf = float(topk)

    def bit_step(it, t):
        cand = t ^ lax.shift_left(jnp.int32(1), 31 - it)
        cnt = jnp.sum(jnp.where(key_ref[...] >= cand, 1.0, 0.0), axis=-1, keepdims=True)
        return jnp.where(cnt >= kf, cand, t)

    int_min = jnp.iinfo(jnp.int32).min
    thr = lax.fori_loop(0, 32, bit_step, jnp.full((tq, 1), int_min, jnp.int32))

    key = key_ref[...]
    gt = key > thr
    eq = key == thr
    need = kf - jnp.sum(jnp.where(gt, 1.0, 0.0), axis=-1, keepdims=True)
    r = lax.broadcasted_iota(jnp.int32, (LANES, LANES), 0)
    c = lax.broadcasted_iota(jnp.int32, (LANES, LANES), 1)
    upper = jnp.where(r < c, 1.0, 0.0).astype(BF16)
    offset = jnp.zeros((tq, 1), F32)
    for b in range(seq // LANES):
        sl = slice(b * LANES, (b + 1) * LANES)
        eq_b = jnp.where(eq[:, sl], 1.0, 0.0)
        prefix = _dot(eq_b.astype(BF16), upper) + offset
        sel = (gt[:, sl] | (eq[:, sl] & (prefix < need))) & visible[:, sl]
        bias_ref[:, sl] = jnp.where(sel, 0.0, NEG)
        offset = offset + jnp.sum(eq_b, axis=-1, keepdims=True)

    bias = bias_ref[...]
    first_half = lane < HEAD_DIM
    for pr in range(B_HEADS // 2):
        sl = slice(pr * LANES, (pr + 1) * LANES)
        q2 = q_ref[:, sl]
        k2 = k_ref[:, sl]
        v2 = v_ref[:, sl]
        halves = []
        for hh in range(2):
            keep = first_half if hh == 0 else jnp.logical_not(first_half)
            s = _dot_nt(jnp.where(keep, q2, jnp.zeros_like(q2)), k2) + bias
            m = jnp.max(s, axis=-1, keepdims=True)
            p = jnp.exp(s - m)
            l = jnp.sum(p, axis=-1, keepdims=True)
            halves.append(_dot(p.astype(BF16), v2) / l)
        o_ref[:, sl] = jnp.where(first_half, halves[0], halves[1]).astype(o_ref.dtype)


def _dsa_call(qkv, idx, iw, batch, seq, topk):
    n = qkv.shape[0]
    nq = seq // DSA_TQ
    width = B_HEADS * HEAD_DIM
    return pl.pallas_call(
        functools.partial(_dsa_kernel, topk=topk),
        grid=(batch, nq),
        in_specs=[
            pl.BlockSpec((DSA_TQ, 2 * LANES), lambda b, i: (b * nq + i, 0)),
            pl.BlockSpec((seq, LANES), lambda b, i: (b, 2)),
            pl.BlockSpec((DSA_TQ, LANES), lambda b, i: (b * nq + i, 0)),
            pl.BlockSpec((DSA_TQ, width), lambda b, i: (b * nq + i, 3)),
            pl.BlockSpec((seq, width), lambda b, i: (b, 4)),
            pl.BlockSpec((seq, width), lambda b, i: (b, 5)),
        ],
        out_specs=pl.BlockSpec((DSA_TQ, width), lambda b, i: (b * nq + i, 0)),
        out_shape=jax.ShapeDtypeStruct((n, width), BF16),
        scratch_shapes=[
            pltpu.VMEM((DSA_TQ, seq), jnp.int32),
            pltpu.VMEM((DSA_TQ, seq), F32),
        ],
        compiler_params=_params("parallel", "arbitrary"),
        name="dsa",
    )(idx, idx, iw, qkv, qkv, qkv)


def _band_attn_kernel(q_ref, k0_ref, k1_ref, k2_ref, v0_ref, v1_ref, v2_ref, tbl_ref, o_ref):
    i = pl.program_id(2)
    tq = q_ref.shape[0]
    q2 = q_ref[...]
    lane = lax.broadcasted_iota(jnp.int32, q2.shape, 1)
    first_half = lane < HEAD_DIM
    k_refs = (k0_ref, k1_ref, k2_ref)
    v_refs = (v0_ref, v1_ref, v2_ref)
    n_blk = len(k_refs)
    halves = []
    for hh in range(2):
        keep = first_half if hh == 0 else jnp.logical_not(first_half)
        qm = jnp.where(keep, q2, jnp.zeros_like(q2))
        s = []
        for d in range(n_blk):
            sd = _dot_nt(qm, k_refs[d][...]) + tbl_ref[hh, :, d * tq:(d + 1) * tq]
            if d < n_blk - 1:
                sd = jnp.where(i + d >= n_blk - 1, sd, NEG)
            s.append(sd)
        m = functools.reduce(jnp.maximum, [jnp.max(sd, axis=-1, keepdims=True) for sd in s])
        p = [jnp.exp(sd - m) for sd in s]
        l = sum(jnp.sum(pd, axis=-1, keepdims=True) for pd in p)
        o = sum(_dot(pd.astype(BF16), v_refs[d][...]) for d, pd in enumerate(p))
        halves.append(o / l)
    o_ref[...] = jnp.where(first_half, halves[0], halves[1]).astype(o_ref.dtype)


def _band_attn_call(qkv, tbl, batch, seq):
    n = qkv.shape[0]
    nq = seq // BAND_TQ
    n_pairs = C_HEADS // 2
    n_blk = C_LEFT_CHUNKS * CHUNK // BAND_TQ + 1

    def kv_spec(d, col0):
        return pl.BlockSpec(
            (BAND_TQ, LANES),
            lambda hp, b, i: (b * nq + jnp.maximum(i - (n_blk - 1) + d, 0), col0 + hp))

    in_specs = [pl.BlockSpec((BAND_TQ, LANES), lambda hp, b, i: (b * nq + i, hp))]
    in_specs += [kv_spec(d, n_pairs) for d in range(n_blk)]
    in_specs += [kv_spec(d, 2 * n_pairs) for d in range(n_blk)]
    in_specs += [pl.BlockSpec((2, BAND_TQ, n_blk * BAND_TQ), lambda hp, b, i: (hp, 0, 0))]
    return pl.pallas_call(
        _band_attn_kernel,
        grid=(n_pairs, batch, nq),
        in_specs=in_specs,
        out_specs=pl.BlockSpec((BAND_TQ, LANES), lambda hp, b, i: (b * nq + i, hp)),
        out_shape=jax.ShapeDtypeStruct((n, D_MODEL), BF16),
        compiler_params=_params("parallel", "parallel", "arbitrary"),
        name="band_attn",
    )(*([qkv] * (1 + 2 * n_blk)), tbl)


def _rope_tables(seq):
    def one(dim):
        inv = ROPE_THETA ** (-jnp.arange(0, dim, 2, dtype=F32) / dim)
        ang = jnp.arange(seq, dtype=F32)[:, None] * inv[None, :]
        cos, sin = jnp.cos(ang), jnp.sin(ang)
        reps = LANES // dim
        return (jnp.tile(jnp.concatenate([cos, cos], axis=-1), (1, reps)),
                jnp.tile(jnp.concatenate([-sin, sin], axis=-1), (1, reps)))
    c64, s64 = one(HEAD_DIM)
    c32, s32 = one(IDX_DIM)
    return jnp.stack([c64, s64, c32, s32])


def _even_w_in_layout(w):
    d = w.shape[0]
    n_qkv = 6 * PROJ_CC
    iq = w[:, n_qkv:n_qkv + IDX_HEADS * IDX_DIM]
    ik = w[:, n_qkv + IDX_HEADS * IDX_DIM:n_qkv + IDX_HEADS * IDX_DIM + IDX_DIM]
    iw = w[:, n_qkv + IDX_HEADS * IDX_DIM + IDX_DIM:]
    return jnp.concatenate(
        [w[:, :n_qkv], iq, jnp.tile(ik, (1, LANES // IDX_DIM)),
         iw, jnp.zeros((d, LANES - IDX_HEADS), w.dtype)], axis=1).astype(BF16)


def _band_bias_table(rel_bias):
    left = C_LEFT_CHUNKS * CHUNK
    window = left + BAND_TQ
    qi = jnp.arange(BAND_TQ)[:, None]
    kj = jnp.arange(window)[None, :]
    dist = left + qi - kj
    bias = rel_bias.astype(F32)[:, jnp.clip(dist, -(CHUNK - 1), MAX_REL) + (CHUNK - 1)]
    qc = (left + qi) // CHUNK
    kc = kj // CHUNK
    in_band = (kc <= qc) & (kc >= qc - C_LEFT_CHUNKS)
    return jnp.where(in_band[None], bias, NEG)


def _ffn_weights(wg, wu, wd):
    d = wg.shape[0]
    nc = D_FF // FFN_FC
    wg = wg.astype(BF16).reshape(d, nc, FFN_FC).transpose(1, 0, 2)
    wu = wu.astype(BF16).reshape(d, nc, FFN_FC).transpose(1, 0, 2)
    wd = wd.astype(BF16).reshape(nc, FFN_FC, d)
    return wg, wu, wd


def kernel(x, norm_g, ffn_wg, ffn_wu, ffn_wd, even_w_in, even_w_out, even_lambda, even_subln,
           odd_w_in, odd_w_out, odd_rel_bias):
    batch, seq, d = x.shape
    depth = norm_g.shape[0]
    topk = min(TOPK_MAX, seq // 4)
    x2d = x.reshape(batch * seq, d)
    tabs = _rope_tables(seq)
    pending = ((), ())
    g_mix = norm_g[0, 3]
    for l in range(depth):
        g = norm_g[l]
        x2d = _ffn_call(x2d, pending[0], pending[1], jnp.stack([g_mix, g[0], g[1]]),
                        *_ffn_weights(ffn_wg[l, 0], ffn_wu[l, 0], ffn_wd[l, 0]))
        if l % 2 == 0:
            e = l // 2
            lambda_init = 0.8 - 0.6 * math.exp(-0.3 * l)
            qkv, idx, iw = _proj_even_call(x2d, g[2:3], _even_w_in_layout(even_w_in[e]), tabs, seq)
            o_a = _diff_attn_call(qkv, even_lambda[e], even_subln[e][None, :], batch, seq, lambda_init)
            o_b = _dsa_call(qkv, idx, iw, batch, seq, topk)
            w_out = even_w_out[e].astype(BF16)
            split = A_HEADS * A_VDIM
            pending = ((o_a, o_b), (w_out[:split], w_out[split:]))
        else:
            o = l // 2
            qkv = _proj_odd_call(x2d, g[2:3], odd_w_in[o].astype(BF16))
            o_c = _band_attn_call(qkv, _band_bias_table(odd_rel_bias[o]), batch, seq)
            pending = ((o_c,), (odd_w_out[o].astype(BF16),))
        g_mix = g[3]
        x2d = _ffn_call(x2d, pending[0], pending[1], jnp.stack([g_mix, g[4], g[5]]),
                        *_ffn_weights(ffn_wg[l, 1], ffn_wu[l, 1], ffn_wd[l, 1]))
        pending = ((), ())
    return x2d.reshape(batch, seq, d)
```

```python
import functools
import math

import jax
import jax.numpy as jnp
from jax import lax
from jax.experimental import pallas as pl
from jax.experimental.pallas import tpu as pltpu

D_MODEL = 1024
CHUNK = 64
HEAD_DIM = 64
ROPE_THETA = 10000.0
EPS = 1e-6
D_FF = 2816
A_HEADS = D_MODEL // (4 * HEAD_DIM)
A_VDIM = 2 * HEAD_DIM
B_HEADS = D_MODEL // (2 * HEAD_DIM)
IDX_HEADS = 8
IDX_DIM = 32
TOPK_MAX = 256
C_HEADS = D_MODEL // HEAD_DIM
C_LEFT_CHUNKS = 8
MAX_REL = 256
REL_TABLE = MAX_REL + CHUNK

LANES = 128
NEG = -1e30
VMEM_LIMIT = 56 * 1024 * 1024

FFN_TM = 512
FFN_FC = 256
PROJ_TM = 512
PROJ_CC = 512
DIFF_TQ = 256
DSA_TQ = 128
BAND_TQ = 256

BF16 = jnp.bfloat16
F32 = jnp.float32


def _dot(a, b):
    return jnp.dot(a, b, preferred_element_type=F32)


def _dot_nt(a, b):
    return lax.dot_general(a, b, (((1,), (1,)), ((), ())), preferred_element_type=F32)


def _rms(xf, g):
    ms = jnp.mean(xf * xf, axis=-1, keepdims=True)
    return xf * lax.rsqrt(ms + EPS) * g


def _params(*sem):
    return pltpu.CompilerParams(dimension_semantics=sem, vmem_limit_bytes=VMEM_LIMIT)


def _ffn_kernel(*refs, n_mix, n_chunks):
    x_ref = refs[0]
    mix_refs = refs[1:1 + n_mix]
    wout_refs = refs[1 + n_mix:1 + 2 * n_mix]
    g_ref, wg_ref, wu_ref, wd_ref, out_ref = refs[1 + 2 * n_mix:]

    x = x_ref[...]
    if n_mix:
        m = _dot(mix_refs[0][...], wout_refs[0][...])
        for a_ref, w_ref in zip(mix_refs[1:], wout_refs[1:]):
            m = m + _dot(a_ref[...], w_ref[...])
        x = x + _rms(m, g_ref[0:1, :])
    h = _rms(x, g_ref[1:2, :]).astype(BF16)
    acc = jnp.zeros(x.shape, F32)
    for c in range(n_chunks):
        a = _dot(h, wg_ref[c])
        b = _dot(h, wu_ref[c])
        t = (a / (1.0 + jnp.exp(-a))) * b
        acc = acc + _dot(t.astype(BF16), wd_ref[c])
    out_ref[...] = x + 0.5 * _rms(acc, g_ref[2:3, :])


def _ffn_call(x2d, mixes, wouts, g3, wg, wu, wd):
    n, d = x2d.shape
    nc = D_FF // FFN_FC
    n_mix = len(mixes)
    row = lambda i: (i, 0)
    const2 = lambda i: (0, 0)
    const3 = lambda i: (0, 0, 0)
    in_specs = [pl.BlockSpec((FFN_TM, d), row)]
    in_specs += [pl.BlockSpec((FFN_TM, m.shape[1]), row) for m in mixes]
    in_specs += [pl.BlockSpec(w.shape, const2) for w in wouts]
    in_specs += [
        pl.BlockSpec(g3.shape, const2),
        pl.BlockSpec((nc, d, FFN_FC), const3),
        pl.BlockSpec((nc, d, FFN_FC), const3),
        pl.BlockSpec((nc, FFN_FC, d), const3),
    ]
    return pl.pallas_call(
        functools.partial(_ffn_kernel, n_mix=n_mix, n_chunks=nc),
        grid=(n // FFN_TM,),
        in_specs=in_specs,
        out_specs=pl.BlockSpec((FFN_TM, d), row),
        out_shape=jax.ShapeDtypeStruct((n, d), F32),
        compiler_params=_params("parallel"),
        name="ffn_mix" if n_mix else "ffn",
    )(x2d, *mixes, *wouts, g3, wg, wu, wd)


def _rope(p, cos_t, sin_t, half):
    lane = lax.broadcasted_iota(jnp.int32, p.shape, 1)
    first = (lane & (2 * half - 1)) < half
    partner = jnp.where(first, pltpu.roll(p, LANES - half, 1), pltpu.roll(p, half, 1))
    return p * cos_t + partner * sin_t


def _proj_even_kernel(x_ref, g_ref, w_ref, tab_ref, qkv_ref, idx_ref, iw_ref):
    h = _rms(x_ref[...], g_ref[0:1, :]).astype(BF16)
    c64, s64, c32, s32 = tab_ref[0], tab_ref[1], tab_ref[2], tab_ref[3]
    q_scale = HEAD_DIM ** -0.5
    for c in range(6):
        p = _dot(h, w_ref[:, c * PROJ_CC:(c + 1) * PROJ_CC])
        for j in range(PROJ_CC // LANES):
            pj = p[:, j * LANES:(j + 1) * LANES]
            if c in (0, 1, 3, 4):
                pj = _rope(pj, c64, s64, HEAD_DIM // 2)
            if c in (0, 3):
                pj = pj * q_scale
            col = c * PROJ_CC + j * LANES
            qkv_ref[:, col:col + LANES] = pj.astype(BF16)
    p = _dot(h, w_ref[:, 6 * PROJ_CC:7 * PROJ_CC])
    for j in range(3):
        pj = _rope(p[:, j * LANES:(j + 1) * LANES], c32, s32, IDX_DIM // 2)
        idx_ref[:, j * LANES:(j + 1) * LANES] = pj.astype(BF16)
    iw_ref[...] = p[:, 3 * LANES:4 * LANES] * (IDX_HEADS ** -0.5 * IDX_DIM ** -0.5)


def _proj_even_call(x2d, g, w, tabs, seq):
    n, d = x2d.shape
    steps_per_seq = seq // PROJ_TM
    return pl.pallas_call(
        _proj_even_kernel,
        grid=(n // PROJ_TM,),
        in_specs=[
            pl.BlockSpec((PROJ_TM, d), lambda i: (i, 0)),
            pl.BlockSpec(g.shape, lambda i: (0, 0)),
            pl.BlockSpec(w.shape, lambda i: (0, 0)),
            pl.BlockSpec((4, PROJ_TM, LANES), lambda i: (0, i % steps_per_seq, 0)),
        ],
        out_specs=[
            pl.BlockSpec((PROJ_TM, 6 * PROJ_CC), lambda i: (i, 0)),
            pl.BlockSpec((PROJ_TM, 3 * LANES), lambda i: (i, 0)),
            pl.BlockSpec((PROJ_TM, LANES), lambda i: (i, 0)),
        ],
        out_shape=[
            jax.ShapeDtypeStruct((n, 6 * PROJ_CC), BF16),
            jax.ShapeDtypeStruct((n, 3 * LANES), BF16),
            jax.ShapeDtypeStruct((n, LANES), F32),
        ],
        compiler_params=_params("parallel"),
        name="proj_even",
    )(x2d, g, w, tabs)


def _proj_odd_kernel(x_ref, g_ref, w_ref, qkv_ref):
    h = _rms(x_ref[...], g_ref[0:1, :]).astype(BF16)
    q_scale = HEAD_DIM ** -0.5
    for c in range(3 * D_MODEL // PROJ_CC):
        p = _dot(h, w_ref[:, c * PROJ_CC:(c + 1) * PROJ_CC])
        if c * PROJ_CC < D_MODEL:
            p = p * q_scale
        qkv_ref[:, c * PROJ_CC:(c + 1) * PROJ_CC] = p.astype(BF16)


def _proj_odd_call(x2d, g, w):
    n, d = x2d.shape
    return pl.pallas_call(
        _proj_odd_kernel,
        grid=(n // PROJ_TM,),
        in_specs=[
            pl.BlockSpec((PROJ_TM, d), lambda i: (i, 0)),
            pl.BlockSpec(g.shape, lambda i: (0, 0)),
            pl.BlockSpec(w.shape, lambda i: (0, 0)),
        ],
        out_specs=pl.BlockSpec((PROJ_TM, 3 * d), lambda i: (i, 0)),
        out_shape=jax.ShapeDtypeStruct((n, 3 * d), BF16),
        compiler_params=_params("parallel"),
        name="proj_odd",
    )(x2d, g, w)


def _diff_attn_tile(c, q_ref, k_ref, v_ref, lam, sub_ref, o_ref, lambda_init):
    tq = q_ref.shape[0]
    left = c * tq
    q = q_ref[...]
    lane = lax.broadcasted_iota(jnp.int32, q.shape, 1)
    zero = jnp.zeros_like(q)
    rowc = lax.broadcasted_iota(jnp.int32, (tq, tq), 0) >> 6
    colc = lax.broadcasted_iota(jnp.int32, (tq, tq), 1) >> 6
    diag_mask = colc <= rowc
    k_diag = k_ref[left:left + tq, :]
    k_left = k_ref[0:left, :] if c else None
    comps = []
    for keep in (lane < HEAD_DIM, lane >= HEAD_DIM):
        qc = jnp.where(keep, q, zero)
        parts = [jnp.where(diag_mask, _dot_nt(qc, k_diag), NEG)]
        if c:
            parts.append(_dot_nt(qc, k_left))
        m = functools.reduce(jnp.maximum, [jnp.max(s, axis=-1, keepdims=True) for s in parts])
        parts = [jnp.exp(s - m) for s in parts]
        l = functools.reduce(jnp.add, [jnp.sum(p, axis=-1, keepdims=True) for p in parts])
        comps.append((parts, l))
    (p0, l0), (p1, l1) = comps
    r0 = 1.0 / l0
    r1 = lam / l1
    w = [(a * r0 - b * r1).astype(BF16) for a, b in zip(p0, p1)]
    o = _dot(w[0], v_ref[left:left + tq, :])
    if c:
        o = o + _dot(w[1], v_ref[0:left, :])
    o = _rms(o, sub_ref[...]) * (1.0 - lambda_init)
    o_ref[...] = o.astype(o_ref.dtype)


def _diff_attn_kernel(q_ref, k_ref, v_ref, lam_ref, sub_ref, o_ref, *, lambda_init):
    i = pl.program_id(2)
    lp = lam_ref[...]
    lam = (jnp.exp(jnp.sum(lp[0:1] * lp[1:2], axis=-1, keepdims=True))
           - jnp.exp(jnp.sum(lp[2:3] * lp[3:4], axis=-1, keepdims=True)) + lambda_init)
    for c in range(k_ref.shape[0] // q_ref.shape[0]):
        @pl.when(i == c)
        def _(c=c):
            _diff_attn_tile(c, q_ref, k_ref, v_ref, lam, sub_ref, o_ref, lambda_init)


def _diff_attn_call(qkv, lam_p, subln, batch, seq, lambda_init):
    n = qkv.shape[0]
    nq = seq // DIFF_TQ
    return pl.pallas_call(
        functools.partial(_diff_attn_kernel, lambda_init=lambda_init),
        grid=(batch, A_HEADS, nq),
        in_specs=[
            pl.BlockSpec((DIFF_TQ, LANES), lambda b, h, i: (b * nq + i, h)),
            pl.BlockSpec((seq, LANES), lambda b, h, i: (b, A_HEADS + h)),
            pl.BlockSpec((seq, LANES), lambda b, h, i: (b, 2 * A_HEADS + h)),
            pl.BlockSpec(lam_p.shape, lambda b, h, i: (0, 0)),
            pl.BlockSpec(subln.shape, lambda b, h, i: (0, 0)),
        ],
        out_specs=pl.BlockSpec((DIFF_TQ, LANES), lambda b, h, i: (b * nq + i, h)),
        out_shape=jax.ShapeDtypeStruct((n, A_HEADS * A_VDIM), BF16),
        compiler_params=_params("parallel", "parallel", "arbitrary"),
        name="diff_attn",
    )(qkv, qkv, qkv, lam_p, subln)


def _dsa_kernel(iq_ref, ik_ref, iw_ref, q_ref, k_ref, v_ref, o_ref, key_ref, bias_ref, *, topk):
    i = pl.program_id(1)
    tq = iq_ref.shape[0]
    seq = ik_ref.shape[0]
    lane = lax.broadcasted_iota(jnp.int32, (tq, LANES), 1)
    rowc = (i * tq + lax.broadcasted_iota(jnp.int32, (tq, seq), 0)) >> 6
    colc = lax.broadcasted_iota(jnp.int32, (tq, seq), 1) >> 6
    visible = colc <= rowc

    ik = ik_ref[...]
    iw = iw_ref[...]
    score = jnp.zeros((tq, seq), F32)
    heads_per_vreg = LANES // IDX_DIM
    for h in range(IDX_HEADS):
        blk = h // heads_per_vreg
        qh = iq_ref[:, blk * LANES:(blk + 1) * LANES]
        qh = jnp.where((lane >> 5) == (h % heads_per_vreg), qh, jnp.zeros_like(qh))
        rel = jnp.maximum(_dot_nt(qh, ik), 0.0)
        score = score + rel * iw[:, h:h + 1]
    score = jnp.where(visible, score + 0.0, -jnp.inf)

    bits = pltpu.bitcast(score, jnp.int32)
    key_ref[...] = bits ^ ((bits >> 31) & 0x7FFFFFFF)

    kf = float(topk)

    def bit_step(it, t):
        cand = t ^ lax.shift_left(jnp.int32(1), 31 - it)
        cnt = jnp.sum(jnp.where(key_ref[...] >= cand, 1.0, 0.0), axis=-1, keepdims=True)
        return jnp.where(cnt >= kf, cand, t)

    int_min = jnp.iinfo(jnp.int32).min
    thr = lax.fori_loop(0, 32, bit_step, jnp.full((tq, 1), int_min, jnp.int32))

    key = key_ref[...]
    gt = key > thr
    eq = key == thr
    need = kf - jnp.sum(jnp.where(gt, 1.0, 0.0), axis=-1, keepdims=True)
    r = lax.broadcasted_iota(jnp.int32, (LANES, LANES), 0)
    c = lax.broadcasted_iota(jnp.int32, (LANES, LANES), 1)
    upper = jnp.where(r < c, 1.0, 0.0).astype(BF16)
    offset = jnp.zeros((tq, 1), F32)
    for b in range(seq // LANES):
        sl = slice(b * LANES, (b + 1) * LANES)
        eq_b = jnp.where(eq[:, sl], 1.0, 0.0)
        prefix = _dot(eq_b.astype(BF16), upper) + offset
        sel = (gt[:, sl] | (eq[:, sl] & (prefix < need))) & visible[:, sl]
        bias_ref[:, sl] = jnp.where(sel, 0.0, NEG)
        offset = offset + jnp.sum(eq_b, axis=-1, keepdims=True)

    bias = bias_ref[...]
    first_half = lane < HEAD_DIM
    for pr in range(B_HEADS // 2):
        sl = slice(pr * LANES, (pr + 1) * LANES)
        q2 = q_ref[:, sl]
        k2 = k_ref[:, sl]
        v2 = v_ref[:, sl]
        halves = []
        for hh in range(2):
            keep = first_half if hh == 0 else jnp.logical_not(first_half)
            s = _dot_nt(jnp.where(keep, q2, jnp.zeros_like(q2)), k2) + bias
            m = jnp.max(s, axis=-1, keepdims=True)
            p = jnp.exp(s - m)
            l = jnp.sum(p, axis=-1, keepdims=True)
            halves.append(_dot(p.astype(BF16), v2) / l)
        o_ref[:, sl] = jnp.where(first_half, halves[0], halves[1]).astype(o_ref.dtype)


def _dsa_call(qkv, idx, iw, batch, seq, topk):
    n = qkv.shape[0]
    nq = seq // DSA_TQ
    width = B_HEADS * HEAD_DIM
    return pl.pallas_call(
        functools.partial(_dsa_kernel, topk=topk),
        grid=(batch, nq),
        in_specs=[
            pl.BlockSpec((DSA_TQ, 2 * LANES), lambda b, i: (b * nq + i, 0)),
            pl.BlockSpec((seq, LANES), lambda b, i: (b, 2)),
            pl.BlockSpec((DSA_TQ, LANES), lambda b, i: (b * nq + i, 0)),
            pl.BlockSpec((DSA_TQ, width), lambda b, i: (b * nq + i, 3)),
            pl.BlockSpec((seq, width), lambda b, i: (b, 4)),
            pl.BlockSpec((seq, width), lambda b, i: (b, 5)),
        ],
        out_specs=pl.BlockSpec((DSA_TQ, width), lambda b, i: (b * nq + i, 0)),
        out_shape=jax.ShapeDtypeStruct((n, width), BF16),
        scratch_shapes=[
            pltpu.VMEM((DSA_TQ, seq), jnp.int32),
            pltpu.VMEM((DSA_TQ, seq), F32),
        ],
        compiler_params=_params("parallel", "arbitrary"),
        name="dsa",
    )(idx, idx, iw, qkv, qkv, qkv)


def _band_attn_kernel(q_ref, k0_ref, k1_ref, k2_ref, v0_ref, v1_ref, v2_ref, tbl_ref, o_ref):
    i = pl.program_id(2)
    tq = q_ref.shape[0]
    q2 = q_ref[...]
    lane = lax.broadcasted_iota(jnp.int32, q2.shape, 1)
    first_half = lane < HEAD_DIM
    k_refs = (k0_ref, k1_ref, k2_ref)
    v_refs = (v0_ref, v1_ref, v2_ref)
    n_blk = len(k_refs)
    halves = []
    for hh in range(2):
        keep = first_half if hh == 0 else jnp.logical_not(first_half)
        qm = jnp.where(keep, q2, jnp.zeros_like(q2))
        s = []
        for d in range(n_blk):
            sd = _dot_nt(qm, k_refs[d][...]) + tbl_ref[hh, :, d * tq:(d + 1) * tq]
            if d < n_blk - 1:
                sd = jnp.where(i + d >= n_blk - 1, sd, NEG)
            s.append(sd)
        m = functools.reduce(jnp.maximum, [jnp.max(sd, axis=-1, keepdims=True) for sd in s])
        p = [jnp.exp(sd - m) for sd in s]
        l = sum(jnp.sum(pd, axis=-1, keepdims=True) for pd in p)
        o = sum(_dot(pd.astype(BF16), v_refs[d][...]) for d, pd in enumerate(p))
        halves.append(o / l)
    o_ref[...] = jnp.where(first_half, halves[0], halves[1]).astype(o_ref.dtype)


def _band_attn_call(qkv, tbl, batch, seq):
    n = qkv.shape[0]
    nq = seq // BAND_TQ
    n_pairs = C_HEADS // 2
    n_blk = C_LEFT_CHUNKS * CHUNK // BAND_TQ + 1

    def kv_spec(d, col0):
        return pl.BlockSpec(
            (BAND_TQ, LANES),
            lambda hp, b, i: (b * nq + jnp.maximum(i - (n_blk - 1) + d, 0), col0 + hp))

    in_specs = [pl.BlockSpec((BAND_TQ, LANES), lambda hp, b, i: (b * nq + i, hp))]
    in_specs += [kv_spec(d, n_pairs) for d in range(n_blk)]
    in_specs += [kv_spec(d, 2 * n_pairs) for d in range(n_blk)]
    in_specs += [pl.BlockSpec((2, BAND_TQ, n_blk * BAND_TQ), lambda hp, b, i: (hp, 0, 0))]
    return pl.pallas_call(
        _band_attn_kernel,
        grid=(n_pairs, batch, nq),
        in_specs=in_specs,
        out_specs=pl.BlockSpec((BAND_TQ, LANES), lambda hp, b, i: (b * nq + i, hp)),
        out_shape=jax.ShapeDtypeStruct((n, D_MODEL), BF16),
        compiler_params=_params("parallel", "parallel", "arbitrary"),
        name="band_attn",
    )(*([qkv] * (1 + 2 * n_blk)), tbl)


def _rope_tables(seq):
    def one(dim):
        inv = ROPE_THETA ** (-jnp.arange(0, dim, 2, dtype=F32) / dim)
        ang = jnp.arange(seq, dtype=F32)[:, None] * inv[None, :]
        cos, sin = jnp.cos(ang), jnp.sin(ang)
        reps = LANES // dim
        return (jnp.tile(jnp.concatenate([cos, cos], axis=-1), (1, reps)),
                jnp.tile(jnp.concatenate([-sin, sin], axis=-1), (1, reps)))
    c64, s64 = one(HEAD_DIM)
    c32, s32 = one(IDX_DIM)
    return jnp.stack([c64, s64, c32, s32])


def _even_w_in_layout(w):
    d = w.shape[0]
    n_qkv = 6 * PROJ_CC
    iq = w[:, n_qkv:n_qkv + IDX_HEADS * IDX_DIM]
    ik = w[:, n_qkv + IDX_HEADS * IDX_DIM:n_qkv + IDX_HEADS * IDX_DIM + IDX_DIM]
    iw = w[:, n_qkv + IDX_HEADS * IDX_DIM + IDX_DIM:]
    return jnp.concatenate(
        [w[:, :n_qkv], iq, jnp.tile(ik, (1, LANES // IDX_DIM)),
         iw, jnp.zeros((d, LANES - IDX_HEADS), w.dtype)], axis=1).astype(BF16)


def _band_table_kernel(g_ref, o_ref):
    tq, window = o_ref.shape[1], o_ref.shape[2]
    span = g_ref.shape[2]
    gb = jnp.broadcast_to(g_ref[0], (tq, span))
    tbl = pltpu.roll(gb, span - (tq - 1), 1, stride=1, stride_axis=0)[:, :window]
    qc = (window - tq + lax.broadcasted_iota(jnp.int32, (tq, window), 0)) >> 6
    kc = lax.broadcasted_iota(jnp.int32, (tq, window), 1) >> 6
    in_band = (kc <= qc) & (kc >= qc - C_LEFT_CHUNKS)
    o_ref[0] = jnp.where(in_band, tbl, NEG)


def _band_bias_table(rel_bias):
    h = rel_bias.shape[0]
    left = C_LEFT_CHUNKS * CHUNK
    window = left + BAND_TQ
    span = BAND_TQ + window
    n_far = BAND_TQ - 1 + left - MAX_REL
    n_near = span - n_far - REL_TABLE
    rel = rel_bias.astype(F32)
    g = jnp.concatenate([jnp.broadcast_to(rel[:, REL_TABLE - 1:], (h, n_far)), rel[:, ::-1],
                         jnp.broadcast_to(rel[:, :1], (h, n_near))], axis=1)
    return pl.pallas_call(
        _band_table_kernel,
        grid=(h,),
        in_specs=[pl.BlockSpec((1, 1, span), lambda i: (i, 0, 0))],
        out_specs=pl.BlockSpec((1, BAND_TQ, window), lambda i: (i, 0, 0)),
        out_shape=jax.ShapeDtypeStruct((h, BAND_TQ, window), F32),
        compiler_params=_params("parallel"),
        name="band_table",
    )(g[:, None, :])


def _ffn_weights(wg, wu, wd):
    d = wg.shape[0]
    nc = D_FF // FFN_FC
    wg = wg.astype(BF16).reshape(d, nc, FFN_FC).transpose(1, 0, 2)
    wu = wu.astype(BF16).reshape(d, nc, FFN_FC).transpose(1, 0, 2)
    wd = wd.astype(BF16).reshape(nc, FFN_FC, d)
    return wg, wu, wd


def kernel(x, norm_g, ffn_wg, ffn_wu, ffn_wd, even_w_in, even_w_out, even_lambda, even_subln,
           odd_w_in, odd_w_out, odd_rel_bias):
    batch, seq, d = x.shape
    depth = norm_g.shape[0]
    topk = min(TOPK_MAX, seq // 4)
    x2d = x.reshape(batch * seq, d)
    tabs = _rope_tables(seq)
    pending = ((), ())
    g_mix = norm_g[0, 3]
    for l in range(depth):
        g = norm_g[l]
        x2d = _ffn_call(x2d, pending[0], pending[1], jnp.stack([g_mix, g[0], g[1]]),
                        *_ffn_weights(ffn_wg[l, 0], ffn_wu[l, 0], ffn_wd[l, 0]))
        if l % 2 == 0:
            e = l // 2
            lambda_init = 0.8 - 0.6 * math.exp(-0.3 * l)
            qkv, idx, iw = _proj_even_call(x2d, g[2:3], _even_w_in_layout(even_w_in[e]), tabs, seq)
            o_a = _diff_attn_call(qkv, even_lambda[e], even_subln[e][None, :], batch, seq, lambda_init)
            o_b = _dsa_call(qkv, idx, iw, batch, seq, topk)
            w_out = even_w_out[e].astype(BF16)
            split = A_HEADS * A_VDIM
            pending = ((o_a, o_b), (w_out[:split], w_out[split:]))
        else:
            o = l // 2
            qkv = _proj_odd_call(x2d, g[2:3], odd_w_in[o].astype(BF16))
            o_c = _band_attn_call(qkv, _band_bias_table(odd_rel_bias[o]), batch, seq)
            pending = ((o_c,), (odd_w_out[o].astype(BF16),))
        g_mix = g[3]
        x2d = _ffn_call(x2d, pending[0], pending[1], jnp.stack([g_mix, g[4], g[5]]),
                        *_ffn_weights(ffn_wg[l, 1], ffn_wu[l, 1], ffn_wd[l, 1]))
        pending = ((), ())
    return x2d.reshape(batch, seq, d)
```

```python
import functools
import math

import jax
import jax.numpy as jnp
from jax import lax
from jax.experimental import pallas as pl
from jax.experimental.pallas import tpu as pltpu

D_MODEL = 1024
CHUNK = 64
HEAD_DIM = 64
ROPE_THETA = 10000.0
EPS = 1e-6
D_FF = 2816
A_HEADS = D_MODEL // (4 * HEAD_DIM)
A_VDIM = 2 * HEAD_DIM
B_HEADS = D_MODEL // (2 * HEAD_DIM)
IDX_HEADS = 8
IDX_DIM = 32
TOPK_MAX = 256
C_HEADS = D_MODEL // HEAD_DIM
C_LEFT_CHUNKS = 8
MAX_REL = 256
REL_TABLE = MAX_REL + CHUNK

LANES = 128
LOG2E = math.log2(math.e)
NEG = -1e30
VMEM_LIMIT = 56 * 1024 * 1024

FFN_TM = 512
FFN_FC = 256
PROJ_TM = 512
PROJ_CC = 512
DIFF_TQ = 256
DSA_TQ = 512
BAND_TQ = 256

BF16 = jnp.bfloat16
F32 = jnp.float32


def _dot(a, b):
    return jnp.dot(a, b, preferred_element_type=F32)


def _dot_nt(a, b):
    return lax.dot_general(a, b, (((1,), (1,)), ((), ())), preferred_element_type=F32)


def _rms(xf, g):
    ms = jnp.mean(xf * xf, axis=-1, keepdims=True)
    return xf * lax.rsqrt(ms + EPS) * g


def _params(*sem):
    return pltpu.CompilerParams(dimension_semantics=sem, vmem_limit_bytes=VMEM_LIMIT)


def _ffn_kernel(*refs, n_mix, n_chunks):
    x_ref = refs[0]
    mix_refs = refs[1:1 + n_mix]
    wout_refs = refs[1 + n_mix:1 + 2 * n_mix]
    g_ref, wg_ref, wu_ref, wd_ref, out_ref = refs[1 + 2 * n_mix:]

    x = x_ref[...]
    if n_mix:
        m = _dot(mix_refs[0][...], wout_refs[0][...])
        for a_ref, w_ref in zip(mix_refs[1:], wout_refs[1:]):
            m = m + _dot(a_ref[...], w_ref[...])
        x = x + _rms(m, g_ref[0:1, :])
    h = _rms(x, g_ref[1:2, :]).astype(BF16)
    acc = jnp.zeros(x.shape, F32)
    for c in range(n_chunks):
        a = _dot(h, wg_ref[c])
        b = _dot(h, wu_ref[c])
        t = (a / (1.0 + jnp.exp(-a))) * b
        acc = acc + _dot(t.astype(BF16), wd_ref[c])
    out_ref[...] = x + 0.5 * _rms(acc, g_ref[2:3, :])


def _ffn_call(x2d, mixes, wouts, g3, wg, wu, wd):
    n, d = x2d.shape
    nc = D_FF // FFN_FC
    n_mix = len(mixes)
    row = lambda i: (i, 0)
    const2 = lambda i: (0, 0)
    const3 = lambda i: (0, 0, 0)
    in_specs = [pl.BlockSpec((FFN_TM, d), row)]
    in_specs += [pl.BlockSpec((FFN_TM, m.shape[1]), row) for m in mixes]
    in_specs += [pl.BlockSpec(w.shape, const2) for w in wouts]
    in_specs += [
        pl.BlockSpec(g3.shape, const2),
        pl.BlockSpec((nc, d, FFN_FC), const3),
        pl.BlockSpec((nc, d, FFN_FC), const3),
        pl.BlockSpec((nc, FFN_FC, d), const3),
    ]
    return pl.pallas_call(
        functools.partial(_ffn_kernel, n_mix=n_mix, n_chunks=nc),
        grid=(n // FFN_TM,),
        in_specs=in_specs,
        out_specs=pl.BlockSpec((FFN_TM, d), row),
        out_shape=jax.ShapeDtypeStruct((n, d), F32),
        compiler_params=_params("parallel"),
        name="ffn_mix" if n_mix else "ffn",
    )(x2d, *mixes, *wouts, g3, wg, wu, wd)


def _rope(p, cos_t, sin_t, half):
    lane = lax.broadcasted_iota(jnp.int32, p.shape, 1)
    first = (lane & (2 * half - 1)) < half
    partner = jnp.where(first, pltpu.roll(p, LANES - half, 1), pltpu.roll(p, half, 1))
    return p * cos_t + partner * sin_t


def _proj_even_kernel(x_ref, g_ref, w_ref, tab_ref, qkv_ref, vt_ref, idx_ref, iwt_ref):
    h = _rms(x_ref[...], g_ref[0:1, :]).astype(BF16)
    c64, s64, c32, s32 = tab_ref[0], tab_ref[1], tab_ref[2], tab_ref[3]
    q_scale = HEAD_DIM ** -0.5
    for c in range(5):
        p = _dot(h, w_ref[:, c * PROJ_CC:(c + 1) * PROJ_CC])
        for j in range(PROJ_CC // LANES):
            pj = p[:, j * LANES:(j + 1) * LANES]
            if c in (0, 1, 3, 4):
                pj = _rope(pj, c64, s64, HEAD_DIM // 2)
            if c == 0:
                pj = pj * q_scale
            if c == 3:
                pj = pj * (q_scale * LOG2E)
            col = c * PROJ_CC + j * LANES
            qkv_ref[:, col:col + LANES] = pj.astype(BF16)
    vt_ref[0] = _dot(h, w_ref[:, 5 * PROJ_CC:6 * PROJ_CC]).T.astype(BF16)
    p = _dot(h, w_ref[:, 6 * PROJ_CC:7 * PROJ_CC])
    for j in range(3):
        pj = _rope(p[:, j * LANES:(j + 1) * LANES], c32, s32, IDX_DIM // 2)
        idx_ref[:, j * LANES:(j + 1) * LANES] = pj.astype(BF16)
    iw = p[:, 3 * LANES:4 * LANES] * (IDX_HEADS ** -0.5 * IDX_DIM ** -0.5)
    iwt_ref[...] = iw.T[0:IDX_HEADS, :]


def _proj_even_call(x2d, g, w, tabs, seq):
    n, d = x2d.shape
    steps_per_seq = seq // PROJ_TM
    return pl.pallas_call(
        _proj_even_kernel,
        grid=(n // PROJ_TM,),
        in_specs=[
            pl.BlockSpec((PROJ_TM, d), lambda i: (i, 0)),
            pl.BlockSpec(g.shape, lambda i: (0, 0)),
            pl.BlockSpec(w.shape, lambda i: (0, 0)),
            pl.BlockSpec((4, PROJ_TM, LANES), lambda i: (0, i % steps_per_seq, 0)),
        ],
        out_specs=[
            pl.BlockSpec((PROJ_TM, 5 * PROJ_CC), lambda i: (i, 0)),
            pl.BlockSpec((1, PROJ_CC, PROJ_TM), lambda i: (i, 0, 0)),
            pl.BlockSpec((PROJ_TM, 3 * LANES), lambda i: (i, 0)),
            pl.BlockSpec((IDX_HEADS, PROJ_TM), lambda i: (0, i)),
        ],
        out_shape=[
            jax.ShapeDtypeStruct((n, 5 * PROJ_CC), BF16),
            jax.ShapeDtypeStruct((n // PROJ_TM, PROJ_CC, PROJ_TM), BF16),
            jax.ShapeDtypeStruct((n, 3 * LANES), BF16),
            jax.ShapeDtypeStruct((IDX_HEADS, n), F32),
        ],
        compiler_params=_params("parallel"),
        name="proj_even",
    )(x2d, g, w, tabs)


def _proj_odd_kernel(x_ref, g_ref, w_ref, qkv_ref):
    h = _rms(x_ref[...], g_ref[0:1, :]).astype(BF16)
    q_scale = HEAD_DIM ** -0.5
    for c in range(3 * D_MODEL // PROJ_CC):
        p = _dot(h, w_ref[:, c * PROJ_CC:(c + 1) * PROJ_CC])
        if c * PROJ_CC < D_MODEL:
            p = p * q_scale
        qkv_ref[:, c * PROJ_CC:(c + 1) * PROJ_CC] = p.astype(BF16)


def _proj_odd_call(x2d, g, w):
    n, d = x2d.shape
    return pl.pallas_call(
        _proj_odd_kernel,
        grid=(n // PROJ_TM,),
        in_specs=[
            pl.BlockSpec((PROJ_TM, d), lambda i: (i, 0)),
            pl.BlockSpec(g.shape, lambda i: (0, 0)),
            pl.BlockSpec(w.shape, lambda i: (0, 0)),
        ],
        out_specs=pl.BlockSpec((PROJ_TM, 3 * d), lambda i: (i, 0)),
        out_shape=jax.ShapeDtypeStruct((n, 3 * d), BF16),
        compiler_params=_params("parallel"),
        name="proj_odd",
    )(x2d, g, w)


def _diff_attn_tile(c, q_ref, k_ref, v_ref, lam, sub_ref, o_ref, lambda_init):
    tq = q_ref.shape[0]
    left = c * tq
    q = q_ref[...]
    lane = lax.broadcasted_iota(jnp.int32, q.shape, 1)
    zero = jnp.zeros_like(q)
    rowc = lax.broadcasted_iota(jnp.int32, (tq, tq), 0) >> 6
    colc = lax.broadcasted_iota(jnp.int32, (tq, tq), 1) >> 6
    diag_mask = colc <= rowc
    k_diag = k_ref[left:left + tq, :]
    k_left = k_ref[0:left, :] if c else None
    comps = []
    for keep in (lane < HEAD_DIM, lane >= HEAD_DIM):
        qc = jnp.where(keep, q, zero)
        parts = [jnp.where(diag_mask, _dot_nt(qc, k_diag), NEG)]
        if c:
            parts.append(_dot_nt(qc, k_left))
        m = functools.reduce(jnp.maximum, [jnp.max(s, axis=-1, keepdims=True) for s in parts])
        parts = [jnp.exp(s - m) for s in parts]
        l = functools.reduce(jnp.add, [jnp.sum(p, axis=-1, keepdims=True) for p in parts])
        comps.append((parts, l))
    (p0, l0), (p1, l1) = comps
    r0 = 1.0 / l0
    r1 = lam / l1
    w = [(a * r0 - b * r1).astype(BF16) for a, b in zip(p0, p1)]
    o = _dot(w[0], v_ref[left:left + tq, :])
    if c:
        o = o + _dot(w[1], v_ref[0:left, :])
    o = _rms(o, sub_ref[...]) * (1.0 - lambda_init)
    o_ref[...] = o.astype(o_ref.dtype)


def _diff_attn_kernel(q_ref, k_ref, v_ref, lam_ref, sub_ref, o_ref, *, lambda_init):
    i = pl.program_id(2)
    lp = lam_ref[...]
    lam = (jnp.exp(jnp.sum(lp[0:1] * lp[1:2], axis=-1, keepdims=True))
           - jnp.exp(jnp.sum(lp[2:3] * lp[3:4], axis=-1, keepdims=True)) + lambda_init)
    for c in range(k_ref.shape[0] // q_ref.shape[0]):
        @pl.when(i == c)
        def _(c=c):
            _diff_attn_tile(c, q_ref, k_ref, v_ref, lam, sub_ref, o_ref, lambda_init)


def _diff_attn_call(qkv, lam_p, subln, batch, seq, lambda_init):
    n = qkv.shape[0]
    nq = seq // DIFF_TQ
    return pl.pallas_call(
        functools.partial(_diff_attn_kernel, lambda_init=lambda_init),
        grid=(batch, A_HEADS, nq),
        in_specs=[
            pl.BlockSpec((DIFF_TQ, LANES), lambda b, h, i: (b * nq + i, h)),
            pl.BlockSpec((seq, LANES), lambda b, h, i: (b, A_HEADS + h)),
            pl.BlockSpec((seq, LANES), lambda b, h, i: (b, 2 * A_HEADS + h)),
            pl.BlockSpec(lam_p.shape, lambda b, h, i: (0, 0)),
            pl.BlockSpec(subln.shape, lambda b, h, i: (0, 0)),
        ],
        out_specs=pl.BlockSpec((DIFF_TQ, LANES), lambda b, h, i: (b * nq + i, h)),
        out_shape=jax.ShapeDtypeStruct((n, A_HEADS * A_VDIM), BF16),
        compiler_params=_params("parallel", "parallel", "arbitrary"),
        name="diff_attn",
    )(qkv, qkv, qkv, lam_p, subln)


def _dsa_kernel(iq_ref, ik_ref, iwt_ref, q_ref, k_ref, vt_ref, o_ref,
                qh_ref, qm_ref, key_ref, hi_ref, lo_ref, bias_ref, acc_ref, *, topk):
    i = pl.program_id(1)
    tq = q_ref.shape[0]
    kb_size = key_ref.shape[1]
    n_kb = i + 1
    kf = float(topk)
    pack = 16
    rows16 = kb_size // pack
    min16 = jnp.int16(-32768)
    one16 = jnp.int16(1)
    zero16 = jnp.int16(0)

    lane = lax.broadcasted_iota(jnp.int32, (tq, LANES), 1)
    idx_per_vreg = LANES // IDX_DIM
    for h in range(IDX_HEADS):
        x = iq_ref[:, (h // idx_per_vreg) * LANES:(h // idx_per_vreg + 1) * LANES]
        qh_ref[h] = jnp.where((lane >> 5) == (h % idx_per_vreg), x, jnp.zeros_like(x))
    for h in range(B_HEADS):
        x = q_ref[:, (h // 2) * LANES:(h // 2 + 1) * LANES]
        qm_ref[h] = jnp.where((lane >> 6) == (h % 2), x, jnp.zeros_like(x))

    q_chunk = (i * tq + lax.broadcasted_iota(jnp.int32, (kb_size, tq), 1)) >> 6

    def visible(kb):
        k_chunk = (kb * kb_size + lax.broadcasted_iota(jnp.int32, (kb_size, tq), 0)) >> 6
        return k_chunk <= q_chunk

    def score_block(kb, carry):
        start = pl.multiple_of(kb * kb_size, kb_size)
        ikb = ik_ref[pl.ds(start, kb_size), :]
        dots = [_dot_nt(ikb, qh_ref[h]) for h in range(IDX_HEADS)]
        score = jnp.maximum(dots[0], 0.0) * iwt_ref[0:1, :]
        for h in range(1, IDX_HEADS):
            score = score + jnp.maximum(dots[h], 0.0) * iwt_ref[h:h + 1, :]
        score = jnp.where(visible(kb), score + 0.0, -jnp.inf)
        bits = pltpu.bitcast(score, jnp.int32)
        key = bits ^ ((bits >> 31) & 0x7FFFFFFF)
        key_ref[kb] = key
        hi_ref[kb] = (key >> 16).astype(jnp.int16)
        lo_ref[kb] = ((key & 0xFFFF) - 32768).astype(jnp.int16)
        return carry

    lax.fori_loop(0, n_kb, score_block, 0)

    def tile16(x):
        return jnp.broadcast_to(x, (pack, tq)).astype(jnp.int16)

    def count16(ref, pred):
        def block(kb, acc):
            for r in range(rows16):
                x = ref[kb, r * pack:(r + 1) * pack, :]
                acc = acc + jnp.where(pred(x), one16, zero16)
            return acc
        acc = lax.fori_loop(0, n_kb, block, jnp.zeros((pack, tq), jnp.int16))
        return jnp.sum(acc.astype(F32), axis=0, keepdims=True)

    def kth_largest16(ref, target):
        def step(it, t):
            cand = t + lax.shift_left(jnp.int32(1), 15 - it)
            c16 = tile16(cand)
            cnt = count16(ref, lambda x: x >= c16)
            return jnp.where(cnt >= target, cand, t)
        return lax.fori_loop(0, 16, step, jnp.full((1, tq), -32768, jnp.int32))

    t_hi = kth_largest16(hi_ref, kf)
    t_hi16 = tile16(t_hi)

    def restrict_low(kb, acc):
        for r in range(rows16):
            sl = slice(r * pack, (r + 1) * pack)
            h16 = hi_ref[kb, sl, :]
            acc = acc + jnp.where(h16 > t_hi16, one16, zero16)
            lo_ref[kb, sl, :] = jnp.where(h16 == t_hi16, lo_ref[kb, sl, :], min16)
        return acc

    above = lax.fori_loop(0, n_kb, restrict_low, jnp.zeros((pack, tq), jnp.int16))
    want_low = kf - jnp.sum(above.astype(F32), axis=0, keepdims=True)
    t_lo = kth_largest16(lo_ref, want_low)
    t_lo16 = tile16(t_lo)
    need = want_low - count16(lo_ref, lambda x: x > t_lo16)
    thr = lax.shift_left(t_hi, 16) | ((t_lo + 32768) & 0xFFFF)

    r_i = lax.broadcasted_iota(jnp.int32, (LANES, LANES), 0)
    c_i = lax.broadcasted_iota(jnp.int32, (LANES, LANES), 1)
    strict_lower = jnp.where(c_i < r_i, 1.0, 0.0).astype(BF16)

    def select_block(kb, offset):
        vis = visible(kb)
        for j in range(kb_size // LANES):
            sl = slice(j * LANES, (j + 1) * LANES)
            kj = key_ref[kb, sl, :]
            eq = kj == thr
            eq_f = jnp.where(eq, 1.0, 0.0)
            before = _dot(strict_lower, eq_f.astype(BF16)) + offset
            sel = ((kj > thr) | (eq & (before < need))) & vis[sl]
            bias_ref[kb, sl, :] = jnp.where(sel, 0.0, NEG)
            offset = offset + jnp.sum(eq_f, axis=0, keepdims=True)
        return offset

    lax.fori_loop(0, n_kb, select_block, jnp.zeros((1, tq), F32))

    acc_ref[...] = jnp.zeros(acc_ref.shape, F32)

    def attn_block(kb, carry):
        ms, ls = carry
        start = pl.multiple_of(kb * kb_size, kb_size)
        bias = bias_ref[kb]
        new_m, new_l = [], []
        scores = []
        for h in range(B_HEADS):
            k2 = k_ref[pl.ds(start, kb_size), (h // 2) * LANES:(h // 2 + 1) * LANES]
            s = _dot_nt(k2, qm_ref[h]) + bias
            scores.append(s)
            new_m.append(jnp.maximum(ms[h], jnp.max(s, axis=0, keepdims=True)))
        for h in range(B_HEADS):
            alpha = jnp.exp2(ms[h] - new_m[h])
            p = jnp.exp2(scores[h] - new_m[h])
            new_l.append(alpha * ls[h] + jnp.sum(p, axis=0, keepdims=True))
            pv = _dot(vt_ref[kb, h * HEAD_DIM:(h + 1) * HEAD_DIM, :], p.astype(BF16))
            acc_ref[h] = alpha * acc_ref[h] + pv
        return tuple(new_m), tuple(new_l)

    m0 = tuple(jnp.full((1, tq), NEG, F32) for _ in range(B_HEADS))
    l0 = tuple(jnp.zeros((1, tq), F32) for _ in range(B_HEADS))
    _, ls = lax.fori_loop(0, n_kb, attn_block, (m0, l0))
    o_t = jnp.concatenate([acc_ref[h] / ls[h] for h in range(B_HEADS)], axis=0)
    o_ref[...] = o_t.T.astype(o_ref.dtype)


def _dsa_call(qkv, vt, idx, iwt, batch, seq, topk):
    n = qkv.shape[0]
    nq = seq // DSA_TQ
    width = B_HEADS * HEAD_DIM
    assert vt.shape[2] == DSA_TQ, "key blocks must match the projection's token tile"
    return pl.pallas_call(
        functools.partial(_dsa_kernel, topk=topk),
        grid=(batch, nq),
        in_specs=[
            pl.BlockSpec((DSA_TQ, 2 * LANES), lambda b, i: (b * nq + i, 0)),
            pl.BlockSpec((seq, LANES), lambda b, i: (b, 2)),
            pl.BlockSpec((IDX_HEADS, DSA_TQ), lambda b, i: (0, b * nq + i)),
            pl.BlockSpec((DSA_TQ, width), lambda b, i: (b * nq + i, 3)),
            pl.BlockSpec((seq, width), lambda b, i: (b, 4)),
            pl.BlockSpec((nq, width, DSA_TQ), lambda b, i: (b, 0, 0)),
        ],
        out_specs=pl.BlockSpec((DSA_TQ, width), lambda b, i: (b * nq + i, 0)),
        out_shape=jax.ShapeDtypeStruct((n, width), BF16),
        scratch_shapes=[
            pltpu.VMEM((IDX_HEADS, DSA_TQ, LANES), BF16),
            pltpu.VMEM((B_HEADS, DSA_TQ, LANES), BF16),
            pltpu.VMEM((nq, DSA_TQ, DSA_TQ), jnp.int32),
            pltpu.VMEM((nq, DSA_TQ, DSA_TQ), jnp.int16),
            pltpu.VMEM((nq, DSA_TQ, DSA_TQ), jnp.int16),
            pltpu.VMEM((nq, DSA_TQ, DSA_TQ), F32),
            pltpu.VMEM((B_HEADS, HEAD_DIM, DSA_TQ), F32),
        ],
        compiler_params=_params("parallel", "arbitrary"),
        name="dsa",
    )(idx, idx, iwt, qkv, qkv, vt)


def _band_attn_kernel(q_ref, k0_ref, k1_ref, k2_ref, v0_ref, v1_ref, v2_ref, tbl_ref, o_ref):
    i = pl.program_id(2)
    tq = q_ref.shape[0]
    q2 = q_ref[...]
    lane = lax.broadcasted_iota(jnp.int32, q2.shape, 1)
    first_half = lane < HEAD_DIM
    k_refs = (k0_ref, k1_ref, k2_ref)
    v_refs = (v0_ref, v1_ref, v2_ref)
    n_blk = len(k_refs)
    halves = []
    for hh in range(2):
        keep = first_half if hh == 0 else jnp.logical_not(first_half)
        qm = jnp.where(keep, q2, jnp.zeros_like(q2))
        s = []
        for d in range(n_blk):
            sd = _dot_nt(qm, k_refs[d][...]) + tbl_ref[hh, :, d * tq:(d + 1) * tq]
            if d < n_blk - 1:
                sd = jnp.where(i + d >= n_blk - 1, sd, NEG)
            s.append(sd)
        m = functools.reduce(jnp.maximum, [jnp.max(sd, axis=-1, keepdims=True) for sd in s])
        p = [jnp.exp(sd - m) for sd in s]
        l = sum(jnp.sum(pd, axis=-1, keepdims=True) for pd in p)
        o = sum(_dot(pd.astype(BF16), v_refs[d][...]) for d, pd in enumerate(p))
        halves.append(o / l)
    o_ref[...] = jnp.where(first_half, halves[0], halves[1]).astype(o_ref.dtype)


def _band_attn_call(qkv, tbl, batch, seq):
    n = qkv.shape[0]
    nq = seq // BAND_TQ
    n_pairs = C_HEADS // 2
    n_blk = C_LEFT_CHUNKS * CHUNK // BAND_TQ + 1

    def kv_spec(d, col0):
        return pl.BlockSpec(
            (BAND_TQ, LANES),
            lambda hp, b, i: (b * nq + jnp.maximum(i - (n_blk - 1) + d, 0), col0 + hp))

    in_specs = [pl.BlockSpec((BAND_TQ, LANES), lambda hp, b, i: (b * nq + i, hp))]
    in_specs += [kv_spec(d, n_pairs) for d in range(n_blk)]
    in_specs += [kv_spec(d, 2 * n_pairs) for d in range(n_blk)]
    in_specs += [pl.BlockSpec((2, BAND_TQ, n_blk * BAND_TQ), lambda hp, b, i: (hp, 0, 0))]
    return pl.pallas_call(
        _band_attn_kernel,
        grid=(n_pairs, batch, nq),
        in_specs=in_specs,
        out_specs=pl.BlockSpec((BAND_TQ, LANES), lambda hp, b, i: (b * nq + i, hp)),
        out_shape=jax.ShapeDtypeStruct((n, D_MODEL), BF16),
        compiler_params=_params("parallel", "parallel", "arbitrary"),
        name="band_attn",
    )(*([qkv] * (1 + 2 * n_blk)), tbl)


def _rope_tables(seq):
    def one(dim):
        inv = ROPE_THETA ** (-jnp.arange(0, dim, 2, dtype=F32) / dim)
        ang = jnp.arange(seq, dtype=F32)[:, None] * inv[None, :]
        cos, sin = jnp.cos(ang), jnp.sin(ang)
        reps = LANES // dim
        return (jnp.tile(jnp.concatenate([cos, cos], axis=-1), (1, reps)),
                jnp.tile(jnp.concatenate([-sin, sin], axis=-1), (1, reps)))
    c64, s64 = one(HEAD_DIM)
    c32, s32 = one(IDX_DIM)
    return jnp.stack([c64, s64, c32, s32])


def _even_w_in_layout(w):
    d = w.shape[0]
    n_qkv = 6 * PROJ_CC
    iq = w[:, n_qkv:n_qkv + IDX_HEADS * IDX_DIM]
    ik = w[:, n_qkv + IDX_HEADS * IDX_DIM:n_qkv + IDX_HEADS * IDX_DIM + IDX_DIM]
    iw = w[:, n_qkv + IDX_HEADS * IDX_DIM + IDX_DIM:]
    return jnp.concatenate(
        [w[:, :n_qkv], iq, jnp.tile(ik, (1, LANES // IDX_DIM)),
         iw, jnp.zeros((d, LANES - IDX_HEADS), w.dtype)], axis=1).astype(BF16)


def _band_table_kernel(g_ref, o_ref):
    tq, window = o_ref.shape[1], o_ref.shape[2]
    span = g_ref.shape[2]
    gb = jnp.broadcast_to(g_ref[0], (tq, span))
    tbl = pltpu.roll(gb, span - (tq - 1), 1, stride=1, stride_axis=0)[:, :window]
    qc = (window - tq + lax.broadcasted_iota(jnp.int32, (tq, window), 0)) >> 6
    kc = lax.broadcasted_iota(jnp.int32, (tq, window), 1) >> 6
    in_band = (kc <= qc) & (kc >= qc - C_LEFT_CHUNKS)
    o_ref[0] = jnp.where(in_band, tbl, NEG)


def _band_bias_table(rel_bias):
    h = rel_bias.shape[0]
    left = C_LEFT_CHUNKS * CHUNK
    window = left + BAND_TQ
    span = BAND_TQ + window
    n_far = BAND_TQ - 1 + left - MAX_REL
    n_near = span - n_far - REL_TABLE
    rel = rel_bias.astype(F32)
    g = jnp.concatenate([jnp.broadcast_to(rel[:, REL_TABLE - 1:], (h, n_far)), rel[:, ::-1],
                         jnp.broadcast_to(rel[:, :1], (h, n_near))], axis=1)
    return pl.pallas_call(
        _band_table_kernel,
        grid=(h,),
        in_specs=[pl.BlockSpec((1, 1, span), lambda i: (i, 0, 0))],
        out_specs=pl.BlockSpec((1, BAND_TQ, window), lambda i: (i, 0, 0)),
        out_shape=jax.ShapeDtypeStruct((h, BAND_TQ, window), F32),
        compiler_params=_params("parallel"),
        name="band_table",
    )(g[:, None, :])


def _ffn_weights(wg, wu, wd):
    d = wg.shape[0]
    nc = D_FF // FFN_FC
    wg = wg.astype(BF16).reshape(d, nc, FFN_FC).transpose(1, 0, 2)
    wu = wu.astype(BF16).reshape(d, nc, FFN_FC).transpose(1, 0, 2)
    wd = wd.astype(BF16).reshape(nc, FFN_FC, d)
    return wg, wu, wd


def kernel(x, norm_g, ffn_wg, ffn_wu, ffn_wd, even_w_in, even_w_out, even_lambda, even_subln,
           odd_w_in, odd_w_out, odd_rel_bias):
    batch, seq, d = x.shape
    depth = norm_g.shape[0]
    topk = min(TOPK_MAX, seq // 4)
    x2d = x.reshape(batch * seq, d)
    tabs = _rope_tables(seq)
    pending = ((), ())
    g_mix = norm_g[0, 3]
    for l in range(depth):
        g = norm_g[l]
        x2d = _ffn_call(x2d, pending[0], pending[1], jnp.stack([g_mix, g[0], g[1]]),
                        *_ffn_weights(ffn_wg[l, 0], ffn_wu[l, 0], ffn_wd[l, 0]))
        if l % 2 == 0:
            e = l // 2
            lambda_init = 0.8 - 0.6 * math.exp(-0.3 * l)
            qkv, vt, idx, iwt = _proj_even_call(x2d, g[2:3], _even_w_in_layout(even_w_in[e]), tabs, seq)
            o_a = _diff_attn_call(qkv, even_lambda[e], even_subln[e][None, :], batch, seq, lambda_init)
            o_b = _dsa_call(qkv, vt, idx, iwt, batch, seq, topk)
            w_out = even_w_out[e].astype(BF16)
            split = A_HEADS * A_VDIM
            pending = ((o_a, o_b), (w_out[:split], w_out[split:]))
        else:
            o = l // 2
            qkv = _proj_odd_call(x2d, g[2:3], odd_w_in[o].astype(BF16))
            o_c = _band_attn_call(qkv, _band_bias_table(odd_rel_bias[o]), batch, seq)
            pending = ((o_c,), (odd_w_out[o].astype(BF16),))
        g_mix = g[3]
        x2d = _ffn_call(x2d, pending[0], pending[1], jnp.stack([g_mix, g[4], g[5]]),
                        *_ffn_weights(ffn_wg[l, 1], ffn_wu[l, 1], ffn_wd[l, 1]))
        pending = ((), ())
    return x2d.reshape(batch, seq, d)
```

```python
import functools
import math

import jax
import jax.numpy as jnp
from jax import lax
from jax.experimental import pallas as pl
from jax.experimental.pallas import tpu as pltpu

D_MODEL = 1024
CHUNK = 64
HEAD_DIM = 64
ROPE_THETA = 10000.0
EPS = 1e-6
D_FF = 2816
A_HEADS = D_MODEL // (4 * HEAD_DIM)
A_VDIM = 2 * HEAD_DIM
B_HEADS = D_MODEL // (2 * HEAD_DIM)
IDX_HEADS = 8
IDX_DIM = 32
TOPK_MAX = 256
C_HEADS = D_MODEL // HEAD_DIM
C_LEFT_CHUNKS = 8
MAX_REL = 256
REL_TABLE = MAX_REL + CHUNK

LANES = 128
LOG2E = math.log2(math.e)
NEG = -1e30
VMEM_LIMIT = 56 * 1024 * 1024

FFN_TM = 512
FFN_FC = 256
PROJ_TM = 512
PROJ_CC = 512
DIFF_TQ = 256
DIFF_HEADS = 2
DSA_TQ = 512
BAND_TQ = 256
BAND_HEADS = 16

BF16 = jnp.bfloat16
F32 = jnp.float32


def _dot(a, b):
    return jnp.dot(a, b, preferred_element_type=F32)


def _dot_nt(a, b):
    return lax.dot_general(a, b, (((1,), (1,)), ((), ())), preferred_element_type=F32)


def _rms(xf, g):
    ms = jnp.mean(xf * xf, axis=-1, keepdims=True)
    return xf * lax.rsqrt(ms + EPS) * g


def _params(*sem):
    return pltpu.CompilerParams(dimension_semantics=sem, vmem_limit_bytes=VMEM_LIMIT)


def _ffn_kernel(*refs, n_mix, n_chunks):
    x_ref = refs[0]
    mix_refs = refs[1:1 + n_mix]
    wout_refs = refs[1 + n_mix:1 + 2 * n_mix]
    g_ref, wg_ref, wu_ref, wd_ref, out_ref = refs[1 + 2 * n_mix:]

    x = x_ref[...]
    if n_mix:
        m = _dot(mix_refs[0][...], wout_refs[0][...])
        for a_ref, w_ref in zip(mix_refs[1:], wout_refs[1:]):
            m = m + _dot(a_ref[...], w_ref[...])
        x = x + _rms(m, g_ref[0:1, :])
    h = _rms(x, g_ref[1:2, :]).astype(BF16)
    acc = jnp.zeros(x.shape, F32)
    for c in range(n_chunks):
        cols = slice(c * FFN_FC, (c + 1) * FFN_FC)
        a = _dot(h, wg_ref[:, cols])
        b = _dot(h, wu_ref[:, cols])
        t = (a / (1.0 + jnp.exp(-a))) * b
        acc = acc + _dot(t.astype(BF16), wd_ref[cols, :])
    out_ref[...] = x + 0.5 * _rms(acc, g_ref[2:3, :])


def _ffn_call(x2d, mixes, wouts, g3, wg, wu, wd):
    n, d = x2d.shape
    nc = D_FF // FFN_FC
    n_mix = len(mixes)
    row = lambda i: (i, 0)
    const2 = lambda i: (0, 0)
    in_specs = [pl.BlockSpec((FFN_TM, d), row)]
    in_specs += [pl.BlockSpec((FFN_TM, m.shape[1]), row) for m in mixes]
    in_specs += [pl.BlockSpec(w.shape, const2) for w in wouts]
    in_specs += [
        pl.BlockSpec(g3.shape, const2),
        pl.BlockSpec(wg.shape, const2),
        pl.BlockSpec(wu.shape, const2),
        pl.BlockSpec(wd.shape, const2),
    ]
    return pl.pallas_call(
        functools.partial(_ffn_kernel, n_mix=n_mix, n_chunks=nc),
        grid=(n // FFN_TM,),
        in_specs=in_specs,
        out_specs=pl.BlockSpec((FFN_TM, d), row),
        out_shape=jax.ShapeDtypeStruct((n, d), F32),
        compiler_params=_params("parallel"),
        name="ffn_mix" if n_mix else "ffn",
    )(x2d, *mixes, *wouts, g3, wg, wu, wd)


def _rope(p, cos_t, sin_t, half):
    lane = lax.broadcasted_iota(jnp.int32, p.shape, 1)
    first = (lane & (2 * half - 1)) < half
    partner = jnp.where(first, pltpu.roll(p, LANES - half, 1), pltpu.roll(p, half, 1))
    return p * cos_t + partner * sin_t


def _proj_even_kernel(x_ref, g_ref, w_ref, tab_ref, qkv_ref, vt_ref, idx_ref, iwt_ref):
    h = _rms(x_ref[...], g_ref[0:1, :]).astype(BF16)
    c64, s64, c32, s32 = tab_ref[0], tab_ref[1], tab_ref[2], tab_ref[3]
    q_scale = HEAD_DIM ** -0.5
    for c in range(5):
        p = _dot(h, w_ref[:, c * PROJ_CC:(c + 1) * PROJ_CC])
        for j in range(PROJ_CC // LANES):
            pj = p[:, j * LANES:(j + 1) * LANES]
            if c in (0, 1, 3, 4):
                pj = _rope(pj, c64, s64, HEAD_DIM // 2)
            if c in (0, 3):
                pj = pj * (q_scale * LOG2E)
            col = c * PROJ_CC + j * LANES
            qkv_ref[:, col:col + LANES] = pj.astype(BF16)
    vt_ref[0] = _dot(h, w_ref[:, 5 * PROJ_CC:6 * PROJ_CC]).T.astype(BF16)
    p = _dot(h, w_ref[:, 6 * PROJ_CC:7 * PROJ_CC])
    for j in range(3):
        pj = _rope(p[:, j * LANES:(j + 1) * LANES], c32, s32, IDX_DIM // 2)
        idx_ref[:, j * LANES:(j + 1) * LANES] = pj.astype(BF16)
    iw = p[:, 3 * LANES:4 * LANES] * (IDX_HEADS ** -0.5 * IDX_DIM ** -0.5)
    iwt_ref[...] = iw.T[0:IDX_HEADS, :]


def _proj_even_call(x2d, g, w, tabs, seq):
    n, d = x2d.shape
    steps_per_seq = seq // PROJ_TM
    return pl.pallas_call(
        _proj_even_kernel,
        grid=(n // PROJ_TM,),
        in_specs=[
            pl.BlockSpec((PROJ_TM, d), lambda i: (i, 0)),
            pl.BlockSpec(g.shape, lambda i: (0, 0)),
            pl.BlockSpec(w.shape, lambda i: (0, 0)),
            pl.BlockSpec((4, PROJ_TM, LANES), lambda i: (0, i % steps_per_seq, 0)),
        ],
        out_specs=[
            pl.BlockSpec((PROJ_TM, 5 * PROJ_CC), lambda i: (i, 0)),
            pl.BlockSpec((1, PROJ_CC, PROJ_TM), lambda i: (i, 0, 0)),
            pl.BlockSpec((PROJ_TM, 3 * LANES), lambda i: (i, 0)),
            pl.BlockSpec((IDX_HEADS, PROJ_TM), lambda i: (0, i)),
        ],
        out_shape=[
            jax.ShapeDtypeStruct((n, 5 * PROJ_CC), BF16),
            jax.ShapeDtypeStruct((n // PROJ_TM, PROJ_CC, PROJ_TM), BF16),
            jax.ShapeDtypeStruct((n, 3 * LANES), BF16),
            jax.ShapeDtypeStruct((IDX_HEADS, n), F32),
        ],
        compiler_params=_params("parallel"),
        name="proj_even",
    )(x2d, g, w, tabs)


def _proj_odd_kernel(x_ref, g_ref, w_ref, qk_ref, vt_ref):
    h = _rms(x_ref[...], g_ref[0:1, :]).astype(BF16)
    q_scale = HEAD_DIM ** -0.5 * LOG2E
    for c in range(2 * D_MODEL // PROJ_CC):
        p = _dot(h, w_ref[:, c * PROJ_CC:(c + 1) * PROJ_CC])
        if c * PROJ_CC < D_MODEL:
            p = p * q_scale
        qk_ref[:, c * PROJ_CC:(c + 1) * PROJ_CC] = p.astype(BF16)
    for c in range(D_MODEL // PROJ_CC):
        col = 2 * D_MODEL + c * PROJ_CC
        vt = _dot(h, w_ref[:, col:col + PROJ_CC]).T.astype(BF16)
        for j in range(vt_ref.shape[0]):
            vt_ref[j, c * PROJ_CC:(c + 1) * PROJ_CC, :] = vt[:, j * BAND_TQ:(j + 1) * BAND_TQ]


def _proj_odd_call(x2d, g, w):
    n, d = x2d.shape
    return pl.pallas_call(
        _proj_odd_kernel,
        grid=(n // PROJ_TM,),
        in_specs=[
            pl.BlockSpec((PROJ_TM, d), lambda i: (i, 0)),
            pl.BlockSpec(g.shape, lambda i: (0, 0)),
            pl.BlockSpec(w.shape, lambda i: (0, 0)),
        ],
        out_specs=[
            pl.BlockSpec((PROJ_TM, 2 * d), lambda i: (i, 0)),
            pl.BlockSpec((PROJ_TM // BAND_TQ, d, BAND_TQ), lambda i: (i, 0, 0)),
        ],
        out_shape=[
            jax.ShapeDtypeStruct((n, 2 * d), BF16),
            jax.ShapeDtypeStruct((n // BAND_TQ, d, BAND_TQ), BF16),
        ],
        compiler_params=_params("parallel"),
        name="proj_odd",
    )(x2d, g, w)


def _diff_attn_tile(c, q_ref, k_ref, v_ref, lam, sub_ref, o_ref, lambda_init):
    tq = q_ref.shape[0]
    n_heads = q_ref.shape[1] // LANES
    left = c * tq
    lane = lax.broadcasted_iota(jnp.int32, (tq, LANES), 1)
    rowc = lax.broadcasted_iota(jnp.int32, (tq, tq), 0) >> 6
    colc = lax.broadcasted_iota(jnp.int32, (tq, tq), 1) >> 6
    diag_mask = colc <= rowc
    raw = []
    for h in range(n_heads):
        sl = slice(h * LANES, (h + 1) * LANES)
        q = q_ref[:, sl]
        k_diag = k_ref[left:left + tq, sl]
        for keep in (lane < HEAD_DIM, lane >= HEAD_DIM):
            qc = jnp.where(keep, q, jnp.zeros_like(q))
            parts = [jnp.where(diag_mask, _dot_nt(qc, k_diag), NEG)]
            if c:
                parts.append(_dot_nt(qc, k_ref[0:left, sl]))
            raw.append(parts)
    comps = []
    for parts in raw:
        m = functools.reduce(jnp.maximum, [jnp.max(s, axis=-1, keepdims=True) for s in parts])
        parts = [jnp.exp2(s - m) for s in parts]
        l = functools.reduce(jnp.add, [jnp.sum(p, axis=-1, keepdims=True) for p in parts])
        comps.append((parts, l))
    for h in range(n_heads):
        sl = slice(h * LANES, (h + 1) * LANES)
        (p0, l0), (p1, l1) = comps[2 * h], comps[2 * h + 1]
        r0 = 1.0 / l0
        r1 = lam / l1
        w = [(a * r0 - b * r1).astype(BF16) for a, b in zip(p0, p1)]
        o = _dot(w[0], v_ref[left:left + tq, sl])
        if c:
            o = o + _dot(w[1], v_ref[0:left, sl])
        o = _rms(o, sub_ref[...]) * (1.0 - lambda_init)
        o_ref[:, sl] = o.astype(o_ref.dtype)


def _diff_attn_kernel(q_ref, k_ref, v_ref, lam_ref, sub_ref, o_ref, *, lambda_init):
    i = pl.program_id(2)
    lp = lam_ref[...]
    lam = (jnp.exp(jnp.sum(lp[0:1] * lp[1:2], axis=-1, keepdims=True))
           - jnp.exp(jnp.sum(lp[2:3] * lp[3:4], axis=-1, keepdims=True)) + lambda_init)
    for c in range(k_ref.shape[0] // q_ref.shape[0]):
        @pl.when(i == c)
        def _(c=c):
            _diff_attn_tile(c, q_ref, k_ref, v_ref, lam, sub_ref, o_ref, lambda_init)


def _diff_attn_call(qkv, lam_p, subln, batch, seq, lambda_init):
    n = qkv.shape[0]
    nq = seq // DIFF_TQ
    n_groups = A_HEADS // DIFF_HEADS
    width = DIFF_HEADS * A_VDIM
    return pl.pallas_call(
        functools.partial(_diff_attn_kernel, lambda_init=lambda_init),
        grid=(batch, n_groups, nq),
        in_specs=[
            pl.BlockSpec((DIFF_TQ, width), lambda b, g, i: (b * nq + i, g)),
            pl.BlockSpec((seq, width), lambda b, g, i: (b, n_groups + g)),
            pl.BlockSpec((seq, width), lambda b, g, i: (b, 2 * n_groups + g)),
            pl.BlockSpec(lam_p.shape, lambda b, g, i: (0, 0)),
            pl.BlockSpec(subln.shape, lambda b, g, i: (0, 0)),
        ],
        out_specs=pl.BlockSpec((DIFF_TQ, width), lambda b, g, i: (b * nq + i, g)),
        out_shape=jax.ShapeDtypeStruct((n, A_HEADS * A_VDIM), BF16),
        compiler_params=_params("parallel", "parallel", "arbitrary"),
        name="diff_attn",
    )(qkv, qkv, qkv, lam_p, subln)


def _dsa_kernel(iq_ref, ik_ref, iwt_ref, q_ref, k_ref, vt_ref, o_ref,
                qh_ref, qm_ref, key_ref, hi_ref, lo_ref, bias_ref, acc_ref, *, topk):
    i = pl.program_id(1)
    tq = q_ref.shape[0]
    kb_size = key_ref.shape[1]
    n_kb = i + 1
    kf = float(topk)
    pack = 16
    rows16 = kb_size // pack
    min16 = jnp.int16(-32768)
    one16 = jnp.int16(1)
    zero16 = jnp.int16(0)

    lane = lax.broadcasted_iota(jnp.int32, (tq, LANES), 1)
    idx_per_vreg = LANES // IDX_DIM
    for h in range(IDX_HEADS):
        x = iq_ref[:, (h // idx_per_vreg) * LANES:(h // idx_per_vreg + 1) * LANES]
        qh_ref[h] = jnp.where((lane >> 5) == (h % idx_per_vreg), x, jnp.zeros_like(x))
    for h in range(B_HEADS):
        x = q_ref[:, (h // 2) * LANES:(h // 2 + 1) * LANES]
        qm_ref[h] = jnp.where((lane >> 6) == (h % 2), x, jnp.zeros_like(x))

    q_chunk = (i * tq + lax.broadcasted_iota(jnp.int32, (kb_size, tq), 1)) >> 6

    def visible(kb):
        k_chunk = (kb * kb_size + lax.broadcasted_iota(jnp.int32, (kb_size, tq), 0)) >> 6
        return k_chunk <= q_chunk

    def score_block(kb, carry):
        start = pl.multiple_of(kb * kb_size, kb_size)
        ikb = ik_ref[pl.ds(start, kb_size), :]
        dots = [_dot_nt(ikb, qh_ref[h]) for h in range(IDX_HEADS)]
        score = jnp.maximum(dots[0], 0.0) * iwt_ref[0:1, :]
        for h in range(1, IDX_HEADS):
            score = score + jnp.maximum(dots[h], 0.0) * iwt_ref[h:h + 1, :]
        score = jnp.where(visible(kb), score + 0.0, -jnp.inf)
        bits = pltpu.bitcast(score, jnp.int32)
        key = bits ^ ((bits >> 31) & 0x7FFFFFFF)
        key_ref[kb] = key
        hi_ref[kb] = (key >> 16).astype(jnp.int16)
        lo_ref[kb] = ((key & 0xFFFF) - 32768).astype(jnp.int16)
        return carry

    lax.fori_loop(0, n_kb, score_block, 0)

    def tile16(x):
        return jnp.broadcast_to(x, (pack, tq)).astype(jnp.int16)

    def count16(ref, pred):
        def block(kb, acc):
            for r in range(rows16):
                x = ref[kb, r * pack:(r + 1) * pack, :]
                acc = acc + jnp.where(pred(x), one16, zero16)
            return acc
        acc = lax.fori_loop(0, n_kb, block, jnp.zeros((pack, tq), jnp.int16))
        return jnp.sum(acc.astype(F32), axis=0, keepdims=True)

    def kth_largest16(ref, target):
        def step(it, t):
            cand = t + lax.shift_left(jnp.int32(1), 15 - it)
            c16 = tile16(cand)
            cnt = count16(ref, lambda x: x >= c16)
            return jnp.where(cnt >= target, cand, t)
        return lax.fori_loop(0, 16, step, jnp.full((1, tq), -32768, jnp.int32))

    t_hi = kth_largest16(hi_ref, kf)
    t_hi16 = tile16(t_hi)

    def restrict_low(kb, acc):
        for r in range(rows16):
            sl = slice(r * pack, (r + 1) * pack)
            h16 = hi_ref[kb, sl, :]
            acc = acc + jnp.where(h16 > t_hi16, one16, zero16)
            lo_ref[kb, sl, :] = jnp.where(h16 == t_hi16, lo_ref[kb, sl, :], min16)
        return acc

    above = lax.fori_loop(0, n_kb, restrict_low, jnp.zeros((pack, tq), jnp.int16))
    want_low = kf - jnp.sum(above.astype(F32), axis=0, keepdims=True)
    t_lo = kth_largest16(lo_ref, want_low)
    t_lo16 = tile16(t_lo)
    need = want_low - count16(lo_ref, lambda x: x > t_lo16)
    thr = lax.shift_left(t_hi, 16) | ((t_lo + 32768) & 0xFFFF)

    r_i = lax.broadcasted_iota(jnp.int32, (LANES, LANES), 0)
    c_i = lax.broadcasted_iota(jnp.int32, (LANES, LANES), 1)
    strict_lower = jnp.where(c_i < r_i, 1.0, 0.0).astype(BF16)

    def select_block(kb, offset):
        vis = visible(kb)
        for j in range(kb_size // LANES):
            sl = slice(j * LANES, (j + 1) * LANES)
            kj = key_ref[kb, sl, :]
            eq = kj == thr
            eq_f = jnp.where(eq, 1.0, 0.0)
            before = _dot(strict_lower, eq_f.astype(BF16)) + offset
            sel = ((kj > thr) | (eq & (before < need))) & vis[sl]
            bias_ref[kb, sl, :] = jnp.where(sel, 0.0, NEG)
            offset = offset + jnp.sum(eq_f, axis=0, keepdims=True)
        return offset

    lax.fori_loop(0, n_kb, select_block, jnp.zeros((1, tq), F32))

    acc_ref[...] = jnp.zeros(acc_ref.shape, F32)

    def attn_block(kb, carry):
        ms, ls = carry
        start = pl.multiple_of(kb * kb_size, kb_size)
        bias = bias_ref[kb]
        new_m, new_l = [], []
        scores = []
        for h in range(B_HEADS):
            k2 = k_ref[pl.ds(start, kb_size), (h // 2) * LANES:(h // 2 + 1) * LANES]
            s = _dot_nt(k2, qm_ref[h]) + bias
            scores.append(s)
            new_m.append(jnp.maximum(ms[h], jnp.max(s, axis=0, keepdims=True)))
        for h in range(B_HEADS):
            alpha = jnp.exp2(ms[h] - new_m[h])
            p = jnp.exp2(scores[h] - new_m[h])
            new_l.append(alpha * ls[h] + jnp.sum(p, axis=0, keepdims=True))
            pv = _dot(vt_ref[kb, h * HEAD_DIM:(h + 1) * HEAD_DIM, :], p.astype(BF16))
            acc_ref[h] = alpha * acc_ref[h] + pv
        return tuple(new_m), tuple(new_l)

    m0 = tuple(jnp.full((1, tq), NEG, F32) for _ in range(B_HEADS))
    l0 = tuple(jnp.zeros((1, tq), F32) for _ in range(B_HEADS))
    _, ls = lax.fori_loop(0, n_kb, attn_block, (m0, l0))
    o_t = jnp.concatenate([acc_ref[h] / ls[h] for h in range(B_HEADS)], axis=0)
    o_ref[...] = o_t.T.astype(o_ref.dtype)


def _dsa_call(qkv, vt, idx, iwt, batch, seq, topk):
    n = qkv.shape[0]
    nq = seq // DSA_TQ
    width = B_HEADS * HEAD_DIM
    assert vt.shape[2] == DSA_TQ, "key blocks must match the projection's token tile"
    return pl.pallas_call(
        functools.partial(_dsa_kernel, topk=topk),
        grid=(batch, nq),
        in_specs=[
            pl.BlockSpec((DSA_TQ, 2 * LANES), lambda b, i: (b * nq + i, 0)),
            pl.BlockSpec((seq, LANES), lambda b, i: (b, 2)),
            pl.BlockSpec((IDX_HEADS, DSA_TQ), lambda b, i: (0, b * nq + i)),
            pl.BlockSpec((DSA_TQ, width), lambda b, i: (b * nq + i, 3)),
            pl.BlockSpec((seq, width), lambda b, i: (b, 4)),
            pl.BlockSpec((nq, width, DSA_TQ), lambda b, i: (b, 0, 0)),
        ],
        out_specs=pl.BlockSpec((DSA_TQ, width), lambda b, i: (b * nq + i, 0)),
        out_shape=jax.ShapeDtypeStruct((n, width), BF16),
        scratch_shapes=[
            pltpu.VMEM((IDX_HEADS, DSA_TQ, LANES), BF16),
            pltpu.VMEM((B_HEADS, DSA_TQ, LANES), BF16),
            pltpu.VMEM((nq, DSA_TQ, DSA_TQ), jnp.int32),
            pltpu.VMEM((nq, DSA_TQ, DSA_TQ), jnp.int16),
            pltpu.VMEM((nq, DSA_TQ, DSA_TQ), jnp.int16),
            pltpu.VMEM((nq, DSA_TQ, DSA_TQ), F32),
            pltpu.VMEM((B_HEADS, HEAD_DIM, DSA_TQ), F32),
        ],
        compiler_params=_params("parallel", "arbitrary"),
        name="dsa",
    )(idx, idx, iwt, qkv, qkv, vt)


def _band_attn_kernel(q_ref, k0_ref, k1_ref, k2_ref, v0_ref, v1_ref, v2_ref, tbl_ref, o_ref):
    i = pl.program_id(2)
    tq = q_ref.shape[0]
    n_heads = tbl_ref.shape[0]
    lane = lax.broadcasted_iota(jnp.int32, (tq, LANES), 1)
    k_refs = (k0_ref, k1_ref, k2_ref)
    v_refs = (v0_ref, v1_ref, v2_ref)
    n_blk = len(k_refs)
    pens = [jnp.where(i + d >= n_blk - 1, 0.0, NEG).astype(BF16) for d in range(n_blk)]
    scores, ms = [], []
    for h in range(n_heads):
        sl = slice((h // 2) * LANES, (h // 2 + 1) * LANES)
        spare = lane == (HEAD_DIM if h % 2 == 0 else 0)
        q2 = q_ref[:, sl]
        qm = jnp.where((lane >> 6) == (h % 2), q2, jnp.where(spare, 1.0, 0.0).astype(BF16))
        parts = []
        for d in range(n_blk):
            kd = jnp.where(spare, pens[d], k_refs[d][:, sl])
            parts.append(_dot_nt(kd, qm) + tbl_ref[h, d * tq:(d + 1) * tq, :])
        scores.append(parts)
        ms.append(functools.reduce(jnp.maximum, [jnp.max(s, axis=0, keepdims=True) for s in parts]))
    outs = []
    for h in range(n_heads):
        ps = [jnp.exp2(s - ms[h]) for s in scores[h]]
        l = functools.reduce(jnp.add, [jnp.sum(p, axis=0, keepdims=True) for p in ps])
        rows = slice(h * HEAD_DIM, (h + 1) * HEAD_DIM)
        pv = functools.reduce(jnp.add, [_dot(v_refs[d][0, rows, :], p.astype(BF16))
                                        for d, p in enumerate(ps)])
        outs.append(pv / l)
    o_ref[...] = jnp.concatenate(outs, axis=0).T.astype(o_ref.dtype)


def _band_attn_call(qk, vt, tbl, batch, seq):
    n = qk.shape[0]
    nq = seq // BAND_TQ
    n_groups = C_HEADS // BAND_HEADS
    width = BAND_HEADS * HEAD_DIM
    n_blk = C_LEFT_CHUNKS * CHUNK // BAND_TQ + 1

    def blk(i, d):
        return jnp.maximum(i - (n_blk - 1) + d, 0)

    in_specs = [pl.BlockSpec((BAND_TQ, width), lambda g, b, i: (b * nq + i, g))]
    in_specs += [pl.BlockSpec((BAND_TQ, width), lambda g, b, i, d=d: (b * nq + blk(i, d), n_groups + g))
                 for d in range(n_blk)]
    in_specs += [pl.BlockSpec((1, width, BAND_TQ), lambda g, b, i, d=d: (b * nq + blk(i, d), g, 0))
                 for d in range(n_blk)]
    in_specs += [pl.BlockSpec((BAND_HEADS, n_blk * BAND_TQ, BAND_TQ), lambda g, b, i: (g, 0, 0))]
    return pl.pallas_call(
        _band_attn_kernel,
        grid=(n_groups, batch, nq),
        in_specs=in_specs,
        out_specs=pl.BlockSpec((BAND_TQ, width), lambda g, b, i: (b * nq + i, g)),
        out_shape=jax.ShapeDtypeStruct((n, D_MODEL), BF16),
        compiler_params=_params("parallel", "parallel", "arbitrary"),
        name="band_attn",
    )(*([qk] * (1 + n_blk)), *([vt] * n_blk), tbl)


def _rope_tables(seq):
    def one(dim):
        inv = ROPE_THETA ** (-jnp.arange(0, dim, 2, dtype=F32) / dim)
        ang = jnp.arange(seq, dtype=F32)[:, None] * inv[None, :]
        cos, sin = jnp.cos(ang), jnp.sin(ang)
        reps = LANES // dim
        return (jnp.tile(jnp.concatenate([cos, cos], axis=-1), (1, reps)),
                jnp.tile(jnp.concatenate([-sin, sin], axis=-1), (1, reps)))
    c64, s64 = one(HEAD_DIM)
    c32, s32 = one(IDX_DIM)
    return jnp.stack([c64, s64, c32, s32])


def _even_w_in_layout(w):
    d = w.shape[0]
    n_qkv = 6 * PROJ_CC
    iq = w[:, n_qkv:n_qkv + IDX_HEADS * IDX_DIM]
    ik = w[:, n_qkv + IDX_HEADS * IDX_DIM:n_qkv + IDX_HEADS * IDX_DIM + IDX_DIM]
    iw = w[:, n_qkv + IDX_HEADS * IDX_DIM + IDX_DIM:]
    return jnp.concatenate(
        [w[:, :n_qkv], iq, jnp.tile(ik, (1, LANES // IDX_DIM)),
         iw, jnp.zeros((d, LANES - IDX_HEADS), w.dtype)], axis=1).astype(BF16)


def _band_table_kernel(g_ref, o_ref):
    window, tq = o_ref.shape[1], o_ref.shape[2]
    span = g_ref.shape[2]
    gb = jnp.broadcast_to(g_ref[0], (window, span))
    tbl = pltpu.roll(gb, span - window, 1, stride=1, stride_axis=0)[:, :tq]
    kc = lax.broadcasted_iota(jnp.int32, (window, tq), 0) >> 6
    qc = (window - tq + lax.broadcasted_iota(jnp.int32, (window, tq), 1)) >> 6
    in_band = (kc <= qc) & (kc >= qc - C_LEFT_CHUNKS)
    o_ref[0] = jnp.where(in_band, tbl * LOG2E, NEG)


def _band_bias_table(rel_bias):
    h = rel_bias.shape[0]
    left = C_LEFT_CHUNKS * CHUNK
    window = left + BAND_TQ
    span = BAND_TQ + window
    n_near = window - left + 1 - CHUNK
    n_far = span - n_near - REL_TABLE
    rel = rel_bias.astype(F32)
    g = jnp.concatenate([jnp.broadcast_to(rel[:, :1], (h, n_near)), rel,
                         jnp.broadcast_to(rel[:, REL_TABLE - 1:], (h, n_far))], axis=1)
    return pl.pallas_call(
        _band_table_kernel,
        grid=(h,),
        in_specs=[pl.BlockSpec((1, 1, span), lambda i: (i, 0, 0))],
        out_specs=pl.BlockSpec((1, window, BAND_TQ), lambda i: (i, 0, 0)),
        out_shape=jax.ShapeDtypeStruct((h, window, BAND_TQ), F32),
        compiler_params=_params("parallel"),
        name="band_table",
    )(g[:, None, :])


def _ffn_weights(wg, wu, wd):
    return wg.astype(BF16), wu.astype(BF16), wd.astype(BF16)


def kernel(x, norm_g, ffn_wg, ffn_wu, ffn_wd, even_w_in, even_w_out, even_lambda, even_subln,
           odd_w_in, odd_w_out, odd_rel_bias):
    batch, seq, d = x.shape
    depth = norm_g.shape[0]
    topk = min(TOPK_MAX, seq // 4)
    x2d = x.reshape(batch * seq, d)
    tabs = _rope_tables(seq)
    pending = ((), ())
    g_mix = norm_g[0, 3]
    for l in range(depth):
        g = norm_g[l]
        x2d = _ffn_call(x2d, pending[0], pending[1], jnp.stack([g_mix, g[0], g[1]]),
                        *_ffn_weights(ffn_wg[l, 0], ffn_wu[l, 0], ffn_wd[l, 0]))
        if l % 2 == 0:
            e = l // 2
            lambda_init = 0.8 - 0.6 * math.exp(-0.3 * l)
            qkv, vt, idx, iwt = _proj_even_call(x2d, g[2:3], _even_w_in_layout(even_w_in[e]), tabs, seq)
            o_a = _diff_attn_call(qkv, even_lambda[e], even_subln[e][None, :], batch, seq, lambda_init)
            o_b = _dsa_call(qkv, vt, idx, iwt, batch, seq, topk)
            w_out = even_w_out[e].astype(BF16)
            split = A_HEADS * A_VDIM
            pending = ((o_a, o_b), (w_out[:split], w_out[split:]))
        else:
            o = l // 2
            qk, vt = _proj_odd_call(x2d, g[2:3], odd_w_in[o].astype(BF16))
            o_c = _band_attn_call(qk, vt, _band_bias_table(odd_rel_bias[o]), batch, seq)
            pending = ((o_c,), (odd_w_out[o].astype(BF16),))
        g_mix = g[3]
        x2d = _ffn_call(x2d, pending[0], pending[1], jnp.stack([g_mix, g[4], g[5]]),
                        *_ffn_weights(ffn_wg[l, 1], ffn_wu[l, 1], ffn_wd[l, 1]))
        pending = ((), ())
    return x2d.reshape(batch, seq, d)
```

```python
import functools
import math

import jax
import jax.numpy as jnp
from jax import lax
from jax.experimental import pallas as pl
from jax.experimental.pallas import tpu as pltpu

D_MODEL = 1024
CHUNK = 64
HEAD_DIM = 64
ROPE_THETA = 10000.0
EPS = 1e-6
D_FF = 2816
A_HEADS = D_MODEL // (4 * HEAD_DIM)
A_VDIM = 2 * HEAD_DIM
B_HEADS = D_MODEL // (2 * HEAD_DIM)
IDX_HEADS = 8
IDX_DIM = 32
TOPK_MAX = 256
C_HEADS = D_MODEL // HEAD_DIM
C_LEFT_CHUNKS = 8
MAX_REL = 256
REL_TABLE = MAX_REL + CHUNK

LANES = 128
LOG2E = math.log2(math.e)
NEG = -1e30
VMEM_LIMIT = 56 * 1024 * 1024

FFN_TM = 512
FFN_FC = 256
PROJ_TM = 512
PROJ_CC = 512
DIFF_TQ = 256
DIFF_HEADS = 2
DSA_TQ = 512
BAND_TQ = 256
BAND_HEADS = 16

BF16 = jnp.bfloat16
F32 = jnp.float32


def _dot(a, b):
    return jnp.dot(a, b, preferred_element_type=F32)


def _dot_nt(a, b):
    return lax.dot_general(a, b, (((1,), (1,)), ((), ())), preferred_element_type=F32)


def _rms(xf, g):
    ms = jnp.mean(xf * xf, axis=-1, keepdims=True)
    return xf * lax.rsqrt(ms + EPS) * g


def _params(*sem):
    return pltpu.CompilerParams(dimension_semantics=sem, vmem_limit_bytes=VMEM_LIMIT)


def _ffn_kernel(*refs, n_mix, n_chunks):
    x_ref = refs[0]
    mix_refs = refs[1:1 + n_mix]
    wout_refs = refs[1 + n_mix:1 + 2 * n_mix]
    g_ref, wg_ref, wu_ref, wd_ref, out_ref = refs[1 + 2 * n_mix:]

    x = x_ref[...]
    if n_mix:
        m = _dot(mix_refs[0][...], wout_refs[0][...])
        for a_ref, w_ref in zip(mix_refs[1:], wout_refs[1:]):
            m = m + _dot(a_ref[...], w_ref[...])
        x = x + _rms(m, g_ref[0:1, :])
    h = _rms(x, g_ref[1:2, :]).astype(BF16)
    acc = jnp.zeros(x.shape, F32)
    for c in range(n_chunks):
        cols = slice(c * FFN_FC, (c + 1) * FFN_FC)
        a = _dot(h, wg_ref[:, cols])
        b = _dot(h, wu_ref[:, cols])
        t = (a / (1.0 + jnp.exp(-a))) * b
        acc = acc + _dot(t.astype(BF16), wd_ref[cols, :])
    out_ref[...] = x + 0.5 * _rms(acc, g_ref[2:3, :])


def _ffn_call(x2d, mixes, wouts, g3, wg, wu, wd, which):
    n, d = x2d.shape
    nc = D_FF // FFN_FC
    n_mix = len(mixes)
    row = lambda i: (i, 0)
    const2 = lambda i: (0, 0)
    in_specs = [pl.BlockSpec((FFN_TM, d), row)]
    in_specs += [pl.BlockSpec((FFN_TM, m.shape[1]), row) for m in mixes]
    in_specs += [pl.BlockSpec(w.shape, const2) for w in wouts]
    in_specs += [
        pl.BlockSpec(g3.shape, const2),
        pl.BlockSpec((None, None) + wg.shape[2:], lambda i: which + (0, 0)),
        pl.BlockSpec((None, None) + wu.shape[2:], lambda i: which + (0, 0)),
        pl.BlockSpec((None, None) + wd.shape[2:], lambda i: which + (0, 0)),
    ]
    return pl.pallas_call(
        functools.partial(_ffn_kernel, n_mix=n_mix, n_chunks=nc),
        grid=(n // FFN_TM,),
        in_specs=in_specs,
        out_specs=pl.BlockSpec((FFN_TM, d), row),
        out_shape=jax.ShapeDtypeStruct((n, d), F32),
        compiler_params=_params("parallel"),
        name="ffn_mix" if n_mix else "ffn",
    )(x2d, *mixes, *wouts, g3, wg, wu, wd)


def _rope(p, cos_t, sin_t, half):
    lane = lax.broadcasted_iota(jnp.int32, p.shape, 1)
    first = (lane & (2 * half - 1)) < half
    partner = jnp.where(first, pltpu.roll(p, LANES - half, 1), pltpu.roll(p, half, 1))
    return p * cos_t + partner * sin_t


def _proj_even_kernel(x_ref, g_ref, w_ref, tab_ref, qk_ref, avt_ref, bvt_ref, idx_ref, iwt_ref):
    h = _rms(x_ref[...], g_ref[0:1, :]).astype(BF16)
    c64, s64, c32, s32 = tab_ref[0], tab_ref[1], tab_ref[2], tab_ref[3]
    q_scale = HEAD_DIM ** -0.5 * LOG2E
    out_col = 0
    for c in (0, 1, 3, 4):
        p = _dot(h, w_ref[:, c * PROJ_CC:(c + 1) * PROJ_CC])
        for j in range(PROJ_CC // LANES):
            pj = _rope(p[:, j * LANES:(j + 1) * LANES], c64, s64, HEAD_DIM // 2)
            if c in (0, 3):
                pj = pj * q_scale
            qk_ref[:, out_col:out_col + LANES] = pj.astype(BF16)
            out_col += LANES
    avt_ref[...] = _dot(h, w_ref[:, 2 * PROJ_CC:3 * PROJ_CC]).T.astype(BF16)
    bvt_ref[0] = _dot(h, w_ref[:, 5 * PROJ_CC:6 * PROJ_CC]).T.astype(BF16)
    p = _dot(h, w_ref[:, 6 * PROJ_CC:7 * PROJ_CC])
    for j in range(3):
        pj = _rope(p[:, j * LANES:(j + 1) * LANES], c32, s32, IDX_DIM // 2)
        idx_ref[:, j * LANES:(j + 1) * LANES] = pj.astype(BF16)
    iw = p[:, 3 * LANES:4 * LANES] * (IDX_HEADS ** -0.5 * IDX_DIM ** -0.5)
    iwt_ref[...] = iw.T[0:IDX_HEADS, :]


def _proj_even_call(x2d, g, w, tabs, seq):
    n, d = x2d.shape
    steps_per_seq = seq // PROJ_TM
    return pl.pallas_call(
        _proj_even_kernel,
        grid=(n // PROJ_TM,),
        in_specs=[
            pl.BlockSpec((PROJ_TM, d), lambda i: (i, 0)),
            pl.BlockSpec(g.shape, lambda i: (0, 0)),
            pl.BlockSpec(w.shape, lambda i: (0, 0)),
            pl.BlockSpec((4, PROJ_TM, LANES), lambda i: (0, i % steps_per_seq, 0)),
        ],
        out_specs=[
            pl.BlockSpec((PROJ_TM, 4 * PROJ_CC), lambda i: (i, 0)),
            pl.BlockSpec((PROJ_CC, PROJ_TM), lambda i: (0, i)),
            pl.BlockSpec((1, PROJ_CC, PROJ_TM), lambda i: (i, 0, 0)),
            pl.BlockSpec((PROJ_TM, 3 * LANES), lambda i: (i, 0)),
            pl.BlockSpec((IDX_HEADS, PROJ_TM), lambda i: (0, i)),
        ],
        out_shape=[
            jax.ShapeDtypeStruct((n, 4 * PROJ_CC), BF16),
            jax.ShapeDtypeStruct((PROJ_CC, n), BF16),
            jax.ShapeDtypeStruct((n // PROJ_TM, PROJ_CC, PROJ_TM), BF16),
            jax.ShapeDtypeStruct((n, 3 * LANES), BF16),
            jax.ShapeDtypeStruct((IDX_HEADS, n), F32),
        ],
        compiler_params=_params("parallel"),
        name="proj_even",
    )(x2d, g, w, tabs)


def _proj_odd_kernel(x_ref, g_ref, w_ref, qk_ref, vt_ref):
    h = _rms(x_ref[...], g_ref[0:1, :]).astype(BF16)
    q_scale = HEAD_DIM ** -0.5 * LOG2E
    for c in range(2 * D_MODEL // PROJ_CC):
        p = _dot(h, w_ref[:, c * PROJ_CC:(c + 1) * PROJ_CC])
        if c * PROJ_CC < D_MODEL:
            p = p * q_scale
        qk_ref[:, c * PROJ_CC:(c + 1) * PROJ_CC] = p.astype(BF16)
    for c in range(D_MODEL // PROJ_CC):
        col = 2 * D_MODEL + c * PROJ_CC
        vt = _dot(h, w_ref[:, col:col + PROJ_CC]).T.astype(BF16)
        for j in range(vt_ref.shape[0]):
            vt_ref[j, c * PROJ_CC:(c + 1) * PROJ_CC, :] = vt[:, j * BAND_TQ:(j + 1) * BAND_TQ]


def _proj_odd_call(x2d, g, w):
    n, d = x2d.shape
    return pl.pallas_call(
        _proj_odd_kernel,
        grid=(n // PROJ_TM,),
        in_specs=[
            pl.BlockSpec((PROJ_TM, d), lambda i: (i, 0)),
            pl.BlockSpec(g.shape, lambda i: (0, 0)),
            pl.BlockSpec(w.shape, lambda i: (0, 0)),
        ],
        out_specs=[
            pl.BlockSpec((PROJ_TM, 2 * d), lambda i: (i, 0)),
            pl.BlockSpec((PROJ_TM // BAND_TQ, d, BAND_TQ), lambda i: (i, 0, 0)),
        ],
        out_shape=[
            jax.ShapeDtypeStruct((n, 2 * d), BF16),
            jax.ShapeDtypeStruct((n // BAND_TQ, d, BAND_TQ), BF16),
        ],
        compiler_params=_params("parallel"),
        name="proj_odd",
    )(x2d, g, w)


def _diff_attn_tile(c, q_ref, k_ref, vt_ref, lam, sub_ref, o_ref, lambda_init):
    tq = q_ref.shape[0]
    n_heads = q_ref.shape[1] // LANES
    left = c * tq
    lane = lax.broadcasted_iota(jnp.int32, (tq, LANES), 1)
    key_chunk = lax.broadcasted_iota(jnp.int32, (tq, tq), 0) >> 6
    qry_chunk = lax.broadcasted_iota(jnp.int32, (tq, tq), 1) >> 6
    diag_mask = key_chunk <= qry_chunk
    raw = []
    for h in range(n_heads):
        sl = slice(h * LANES, (h + 1) * LANES)
        q = q_ref[:, sl]
        for keep in (lane < HEAD_DIM, lane >= HEAD_DIM):
            qc = jnp.where(keep, q, jnp.zeros_like(q))
            parts = [jnp.where(diag_mask, _dot_nt(k_ref[left:left + tq, sl], qc), NEG)]
            if c:
                parts.append(_dot_nt(k_ref[0:left, sl], qc))
            raw.append(parts)
    comps = []
    for parts in raw:
        m = functools.reduce(jnp.maximum, [jnp.max(s, axis=0, keepdims=True) for s in parts])
        parts = [jnp.exp2(s - m) for s in parts]
        l = functools.reduce(jnp.add, [jnp.sum(p, axis=0, keepdims=True) for p in parts])
        comps.append((parts, l))
    for h in range(n_heads):
        rows = slice(h * A_VDIM, (h + 1) * A_VDIM)
        outs = []
        for parts, l in (comps[2 * h], comps[2 * h + 1]):
            acc = _dot(vt_ref[rows, left:left + tq], parts[0].astype(BF16))
            if c:
                acc = acc + _dot(vt_ref[rows, 0:left], parts[1].astype(BF16))
            outs.append(acc / l)
        o_t = outs[0] - lam * outs[1]
        ms = jnp.mean(o_t * o_t, axis=0, keepdims=True)
        o_t = o_t * lax.rsqrt(ms + EPS) * sub_ref[...] * (1.0 - lambda_init)
        o_ref[:, h * LANES:(h + 1) * LANES] = o_t.T.astype(o_ref.dtype)


def _diff_attn_kernel(q_ref, k_ref, vt_ref, lam_ref, sub_ref, o_ref, *, lambda_init):
    i = pl.program_id(2)
    lp = lam_ref[...]
    lam = (jnp.exp(jnp.sum(lp[0:1] * lp[1:2], axis=-1, keepdims=True))
           - jnp.exp(jnp.sum(lp[2:3] * lp[3:4], axis=-1, keepdims=True)) + lambda_init)
    for c in range(k_ref.shape[0] // q_ref.shape[0]):
        @pl.when(i == c)
        def _(c=c):
            _diff_attn_tile(c, q_ref, k_ref, vt_ref, lam, sub_ref, o_ref, lambda_init)


def _diff_attn_call(qk, avt, lam_p, subln, batch, seq, lambda_init):
    n = qk.shape[0]
    nq = seq // DIFF_TQ
    n_groups = A_HEADS // DIFF_HEADS
    width = DIFF_HEADS * A_VDIM
    sub_t = jnp.broadcast_to(subln.astype(F32)[:, None], (A_VDIM, DIFF_TQ))
    return pl.pallas_call(
        functools.partial(_diff_attn_kernel, lambda_init=lambda_init),
        grid=(batch, n_groups, nq),
        in_specs=[
            pl.BlockSpec((DIFF_TQ, width), lambda b, g, i: (b * nq + i, g)),
            pl.BlockSpec((seq, width), lambda b, g, i: (b, n_groups + g)),
            pl.BlockSpec((width, seq), lambda b, g, i: (g, b)),
            pl.BlockSpec(lam_p.shape, lambda b, g, i: (0, 0)),
            pl.BlockSpec(sub_t.shape, lambda b, g, i: (0, 0)),
        ],
        out_specs=pl.BlockSpec((DIFF_TQ, width), lambda b, g, i: (b * nq + i, g)),
        out_shape=jax.ShapeDtypeStruct((n, A_HEADS * A_VDIM), BF16),
        compiler_params=_params("parallel", "parallel", "arbitrary"),
        name="diff_attn",
    )(qk, qk, avt, lam_p, sub_t)


def _dsa_kernel(iq_ref, ik_ref, iwt_ref, q_ref, k_ref, vt_ref, o_ref,
                qh_ref, qm_ref, key_ref, hi_ref, lo_ref, bias_ref, acc_ref, *, topk):
    i = pl.program_id(1)
    tq = q_ref.shape[0]
    kb_size = key_ref.shape[1]
    n_kb = i + 1
    kf = float(topk)
    pack = 16
    rows16 = kb_size // pack
    min16 = jnp.int16(-32768)
    one16 = jnp.int16(1)
    zero16 = jnp.int16(0)

    lane = lax.broadcasted_iota(jnp.int32, (tq, LANES), 1)
    idx_per_vreg = LANES // IDX_DIM
    for h in range(IDX_HEADS):
        x = iq_ref[:, (h // idx_per_vreg) * LANES:(h // idx_per_vreg + 1) * LANES]
        qh_ref[h] = jnp.where((lane >> 5) == (h % idx_per_vreg), x, jnp.zeros_like(x))
    for h in range(B_HEADS):
        x = q_ref[:, (h // 2) * LANES:(h // 2 + 1) * LANES]
        qm_ref[h] = jnp.where((lane >> 6) == (h % 2), x, jnp.zeros_like(x))

    q_chunk = (i * tq + lax.broadcasted_iota(jnp.int32, (kb_size, tq), 1)) >> 6

    def visible(kb):
        k_chunk = (kb * kb_size + lax.broadcasted_iota(jnp.int32, (kb_size, tq), 0)) >> 6
        return k_chunk <= q_chunk

    def score_block(kb, carry):
        start = pl.multiple_of(kb * kb_size, kb_size)
        ikb = ik_ref[pl.ds(start, kb_size), :]
        dots = [_dot_nt(ikb, qh_ref[h]) for h in range(IDX_HEADS)]
        score = jnp.maximum(dots[0], 0.0) * iwt_ref[0:1, :]
        for h in range(1, IDX_HEADS):
            score = score + jnp.maximum(dots[h], 0.0) * iwt_ref[h:h + 1, :]
        score = jnp.where(visible(kb), score + 0.0, -jnp.inf)
        bits = pltpu.bitcast(score, jnp.int32)
        key = bits ^ ((bits >> 31) & 0x7FFFFFFF)
        key_ref[kb] = key
        hi_ref[kb] = (key >> 16).astype(jnp.int16)
        lo_ref[kb] = ((key & 0xFFFF) - 32768).astype(jnp.int16)
        return carry

    lax.fori_loop(0, n_kb, score_block, 0)

    def tile16(x):
        return jnp.broadcast_to(x, (pack, tq)).astype(jnp.int16)

    def count16(ref, pred):
        def block(kb, acc):
            for r in range(rows16):
                x = ref[kb, r * pack:(r + 1) * pack, :]
                acc = acc + jnp.where(pred(x), one16, zero16)
            return acc
        acc = lax.fori_loop(0, n_kb, block, jnp.zeros((pack, tq), jnp.int16))
        return jnp.sum(acc.astype(F32), axis=0, keepdims=True)

    def kth_largest16(ref, target):
        def step(it, t):
            cand = t + lax.shift_left(jnp.int32(1), 15 - it)
            c16 = tile16(cand)
            cnt = count16(ref, lambda x: x >= c16)
            return jnp.where(cnt >= target, cand, t)
        return lax.fori_loop(0, 16, step, jnp.full((1, tq), -32768, jnp.int32))

    t_hi = kth_largest16(hi_ref, kf)
    t_hi16 = tile16(t_hi)

    def restrict_low(kb, acc):
        for r in range(rows16):
            sl = slice(r * pack, (r + 1) * pack)
            h16 = hi_ref[kb, sl, :]
            acc = acc + jnp.where(h16 > t_hi16, one16, zero16)
            lo_ref[kb, sl, :] = jnp.where(h16 == t_hi16, lo_ref[kb, sl, :], min16)
        return acc

    above = lax.fori_loop(0, n_kb, restrict_low, jnp.zeros((pack, tq), jnp.int16))
    want_low = kf - jnp.sum(above.astype(F32), axis=0, keepdims=True)
    t_lo = kth_largest16(lo_ref, want_low)
    t_lo16 = tile16(t_lo)
    need = want_low - count16(lo_ref, lambda x: x > t_lo16)
    thr = lax.shift_left(t_hi, 16) | ((t_lo + 32768) & 0xFFFF)

    r_i = lax.broadcasted_iota(jnp.int32, (LANES, LANES), 0)
    c_i = lax.broadcasted_iota(jnp.int32, (LANES, LANES), 1)
    strict_lower = jnp.where(c_i < r_i, 1.0, 0.0).astype(BF16)

    def select_block(kb, offset):
        vis = visible(kb)
        for j in range(kb_size // LANES):
            sl = slice(j * LANES, (j + 1) * LANES)
            kj = key_ref[kb, sl, :]
            eq = kj == thr
            eq_f = jnp.where(eq, 1.0, 0.0)
            before = _dot(strict_lower, eq_f.astype(BF16)) + offset
            sel = ((kj > thr) | (eq & (before < need))) & vis[sl]
            bias_ref[kb, sl, :] = jnp.where(sel, 0.0, NEG)
            offset = offset + jnp.sum(eq_f, axis=0, keepdims=True)
        return offset

    lax.fori_loop(0, n_kb, select_block, jnp.zeros((1, tq), F32))

    acc_ref[...] = jnp.zeros(acc_ref.shape, F32)

    def attn_block(kb, carry):
        ms, ls = carry
        start = pl.multiple_of(kb * kb_size, kb_size)
        bias = bias_ref[kb]
        new_m, new_l = [], []
        scores = []
        for h in range(B_HEADS):
            k2 = k_ref[pl.ds(start, kb_size), (h // 2) * LANES:(h // 2 + 1) * LANES]
            s = _dot_nt(k2, qm_ref[h]) + bias
            scores.append(s)
            new_m.append(jnp.maximum(ms[h], jnp.max(s, axis=0, keepdims=True)))
        for h in range(B_HEADS):
            alpha = jnp.exp2(ms[h] - new_m[h])
            p = jnp.exp2(scores[h] - new_m[h])
            new_l.append(alpha * ls[h] + jnp.sum(p, axis=0, keepdims=True))
            pv = _dot(vt_ref[kb, h * HEAD_DIM:(h + 1) * HEAD_DIM, :], p.astype(BF16))
            acc_ref[h] = alpha * acc_ref[h] + pv
        return tuple(new_m), tuple(new_l)

    m0 = tuple(jnp.full((1, tq), NEG, F32) for _ in range(B_HEADS))
    l0 = tuple(jnp.zeros((1, tq), F32) for _ in range(B_HEADS))
    _, ls = lax.fori_loop(0, n_kb, attn_block, (m0, l0))
    o_t = jnp.concatenate([acc_ref[h] / ls[h] for h in range(B_HEADS)], axis=0)
    o_ref[...] = o_t.T.astype(o_ref.dtype)


def _dsa_call(qk, vt, idx, iwt, batch, seq, topk):
    n = qk.shape[0]
    nq = seq // DSA_TQ
    width = B_HEADS * HEAD_DIM
    assert vt.shape[2] == DSA_TQ, "key blocks must match the projection's token tile"
    return pl.pallas_call(
        functools.partial(_dsa_kernel, topk=topk),
        grid=(batch, nq),
        in_specs=[
            pl.BlockSpec((DSA_TQ, 2 * LANES), lambda b, i: (b * nq + i, 0)),
            pl.BlockSpec((seq, LANES), lambda b, i: (b, 2)),
            pl.BlockSpec((IDX_HEADS, DSA_TQ), lambda b, i: (0, b * nq + i)),
            pl.BlockSpec((DSA_TQ, width), lambda b, i: (b * nq + i, 2)),
            pl.BlockSpec((seq, width), lambda b, i: (b, 3)),
            pl.BlockSpec((nq, width, DSA_TQ), lambda b, i: (b, 0, 0)),
        ],
        out_specs=pl.BlockSpec((DSA_TQ, width), lambda b, i: (b * nq + i, 0)),
        out_shape=jax.ShapeDtypeStruct((n, width), BF16),
        scratch_shapes=[
            pltpu.VMEM((IDX_HEADS, DSA_TQ, LANES), BF16),
            pltpu.VMEM((B_HEADS, DSA_TQ, LANES), BF16),
            pltpu.VMEM((nq, DSA_TQ, DSA_TQ), jnp.int32),
            pltpu.VMEM((nq, DSA_TQ, DSA_TQ), jnp.int16),
            pltpu.VMEM((nq, DSA_TQ, DSA_TQ), jnp.int16),
            pltpu.VMEM((nq, DSA_TQ, DSA_TQ), F32),
            pltpu.VMEM((B_HEADS, HEAD_DIM, DSA_TQ), F32),
        ],
        compiler_params=_params("parallel", "arbitrary"),
        name="dsa",
    )(idx, idx, iwt, qk, qk, vt)


def _band_attn_kernel(q_ref, k0_ref, k1_ref, k2_ref, v0_ref, v1_ref, v2_ref, tbl_ref, o_ref):
    i = pl.program_id(2)
    tq = q_ref.shape[0]
    n_heads = tbl_ref.shape[0]
    lane = lax.broadcasted_iota(jnp.int32, (tq, LANES), 1)
    k_refs = (k0_ref, k1_ref, k2_ref)
    v_refs = (v0_ref, v1_ref, v2_ref)
    n_blk = len(k_refs)
    pens = [jnp.where(i + d >= n_blk - 1, 0.0, NEG).astype(BF16) for d in range(n_blk)]
    scores, ms = [], []
    for h in range(n_heads):
        sl = slice((h // 2) * LANES, (h // 2 + 1) * LANES)
        spare = lane == (HEAD_DIM if h % 2 == 0 else 0)
        q2 = q_ref[:, sl]
        qm = jnp.where((lane >> 6) == (h % 2), q2, jnp.where(spare, 1.0, 0.0).astype(BF16))
        parts = []
        for d in range(n_blk):
            kd = jnp.where(spare, pens[d], k_refs[d][:, sl])
            parts.append(_dot_nt(kd, qm) + tbl_ref[h, d * tq:(d + 1) * tq, :])
        scores.append(parts)
        ms.append(functools.reduce(jnp.maximum, [jnp.max(s, axis=0, keepdims=True) for s in parts]))
    outs = []
    for h in range(n_heads):
        ps = [jnp.exp2(s - ms[h]) for s in scores[h]]
        l = functools.reduce(jnp.add, [jnp.sum(p, axis=0, keepdims=True) for p in ps])
        rows = slice(h * HEAD_DIM, (h + 1) * HEAD_DIM)
        pv = functools.reduce(jnp.add, [_dot(v_refs[d][0, rows, :], p.astype(BF16))
                                        for d, p in enumerate(ps)])
        outs.append(pv / l)
    o_ref[...] = jnp.concatenate(outs, axis=0).T.astype(o_ref.dtype)


def _band_attn_call(qk, vt, tbl, batch, seq):
    n = qk.shape[0]
    nq = seq // BAND_TQ
    n_groups = C_HEADS // BAND_HEADS
    width = BAND_HEADS * HEAD_DIM
    n_blk = C_LEFT_CHUNKS * CHUNK // BAND_TQ + 1

    def blk(i, d):
        return jnp.maximum(i - (n_blk - 1) + d, 0)

    in_specs = [pl.BlockSpec((BAND_TQ, width), lambda g, b, i: (b * nq + i, g))]
    in_specs += [pl.BlockSpec((BAND_TQ, width), lambda g, b, i, d=d: (b * nq + blk(i, d), n_groups + g))
                 for d in range(n_blk)]
    in_specs += [pl.BlockSpec((1, width, BAND_TQ), lambda g, b, i, d=d: (b * nq + blk(i, d), g, 0))
                 for d in range(n_blk)]
    in_specs += [pl.BlockSpec((BAND_HEADS, n_blk * BAND_TQ, BAND_TQ), lambda g, b, i: (g, 0, 0))]
    return pl.pallas_call(
        _band_attn_kernel,
        grid=(n_groups, batch, nq),
        in_specs=in_specs,
        out_specs=pl.BlockSpec((BAND_TQ, width), lambda g, b, i: (b * nq + i, g)),
        out_shape=jax.ShapeDtypeStruct((n, D_MODEL), BF16),
        compiler_params=_params("parallel", "parallel", "arbitrary"),
        name="band_attn",
    )(*([qk] * (1 + n_blk)), *([vt] * n_blk), tbl)


def _rope_tables(seq):
    def one(dim):
        inv = ROPE_THETA ** (-jnp.arange(0, dim, 2, dtype=F32) / dim)
        ang = jnp.arange(seq, dtype=F32)[:, None] * inv[None, :]
        cos, sin = jnp.cos(ang), jnp.sin(ang)
        reps = LANES // dim
        return (jnp.tile(jnp.concatenate([cos, cos], axis=-1), (1, reps)),
                jnp.tile(jnp.concatenate([-sin, sin], axis=-1), (1, reps)))
    c64, s64 = one(HEAD_DIM)
    c32, s32 = one(IDX_DIM)
    return jnp.stack([c64, s64, c32, s32])


def _even_w_in_layout(w):
    d = w.shape[0]
    n_qkv = 6 * PROJ_CC
    iq = w[:, n_qkv:n_qkv + IDX_HEADS * IDX_DIM]
    ik = w[:, n_qkv + IDX_HEADS * IDX_DIM:n_qkv + IDX_HEADS * IDX_DIM + IDX_DIM]
    iw = w[:, n_qkv + IDX_HEADS * IDX_DIM + IDX_DIM:]
    return jnp.concatenate(
        [w[:, :n_qkv], iq, jnp.tile(ik, (1, LANES // IDX_DIM)),
         iw, jnp.zeros((d, LANES - IDX_HEADS), w.dtype)], axis=1).astype(BF16)


def _band_table_kernel(g_ref, o_ref):
    window, tq = o_ref.shape[1], o_ref.shape[2]
    span = g_ref.shape[2]
    gb = jnp.broadcast_to(g_ref[0], (window, span))
    tbl = pltpu.roll(gb, span - window, 1, stride=1, stride_axis=0)[:, :tq]
    kc = lax.broadcasted_iota(jnp.int32, (window, tq), 0) >> 6
    qc = (window - tq + lax.broadcasted_iota(jnp.int32, (window, tq), 1)) >> 6
    in_band = (kc <= qc) & (kc >= qc - C_LEFT_CHUNKS)
    o_ref[0] = jnp.where(in_band, tbl * LOG2E, NEG)


def _band_bias_table(rel_bias):
    h = rel_bias.shape[0]
    left = C_LEFT_CHUNKS * CHUNK
    window = left + BAND_TQ
    span = BAND_TQ + window
    n_near = window - left + 1 - CHUNK
    n_far = span - n_near - REL_TABLE
    rel = rel_bias.astype(F32)
    g = jnp.concatenate([jnp.broadcast_to(rel[:, :1], (h, n_near)), rel,
                         jnp.broadcast_to(rel[:, REL_TABLE - 1:], (h, n_far))], axis=1)
    return pl.pallas_call(
        _band_table_kernel,
        grid=(h,),
        in_specs=[pl.BlockSpec((1, 1, span), lambda i: (i, 0, 0))],
        out_specs=pl.BlockSpec((1, window, BAND_TQ), lambda i: (i, 0, 0)),
        out_shape=jax.ShapeDtypeStruct((h, window, BAND_TQ), F32),
        compiler_params=_params("parallel"),
        name="band_table",
    )(g[:, None, :])


def kernel(x, norm_g, ffn_wg, ffn_wu, ffn_wd, even_w_in, even_w_out, even_lambda, even_subln,
           odd_w_in, odd_w_out, odd_rel_bias):
    batch, seq, d = x.shape
    depth = norm_g.shape[0]
    topk = min(TOPK_MAX, seq // 4)
    x2d = x.reshape(batch * seq, d)
    tabs = _rope_tables(seq)
    ffn_w = (ffn_wg.astype(BF16), ffn_wu.astype(BF16), ffn_wd.astype(BF16))
    pending = ((), ())
    g_mix = norm_g[0, 3]
    for l in range(depth):
        g = norm_g[l]
        x2d = _ffn_call(x2d, pending[0], pending[1], jnp.stack([g_mix, g[0], g[1]]), *ffn_w, (l, 0))
        if l % 2 == 0:
            e = l // 2
            lambda_init = 0.8 - 0.6 * math.exp(-0.3 * l)
            qk, avt, bvt, idx, iwt = _proj_even_call(x2d, g[2:3], _even_w_in_layout(even_w_in[e]), tabs, seq)
            o_a = _diff_attn_call(qk, avt, even_lambda[e], even_subln[e], batch, seq, lambda_init)
            o_b = _dsa_call(qk, bvt, idx, iwt, batch, seq, topk)
            w_out = even_w_out[e].astype(BF16)
            split = A_HEADS * A_VDIM
            pending = ((o_a, o_b), (w_out[:split], w_out[split:]))
        else:
            o = l // 2
            qk, vt = _proj_odd_call(x2d, g[2:3], odd_w_in[o].astype(BF16))
            o_c = _band_attn_call(qk, vt, _band_bias_table(odd_rel_bias[o]), batch, seq)
            pending = ((o_c,), (odd_w_out[o].astype(BF16),))
        g_mix = g[3]
        x2d = _ffn_call(x2d, pending[0], pending[1], jnp.stack([g_mix, g[4], g[5]]), *ffn_w, (l, 1))
        pending = ((), ())
    return x2d.reshape(batch, seq, d)
```

```python
import functools
import math

import jax
import jax.numpy as jnp
from jax import lax
from jax.experimental import pallas as pl
from jax.experimental.pallas import tpu as pltpu

D_MODEL = 1024
CHUNK = 64
HEAD_DIM = 64
ROPE_THETA = 10000.0
EPS = 1e-6
D_FF = 2816
A_HEADS = D_MODEL // (4 * HEAD_DIM)
A_VDIM = 2 * HEAD_DIM
B_HEADS = D_MODEL // (2 * HEAD_DIM)
IDX_HEADS = 8
IDX_DIM = 32
TOPK_MAX = 256
C_HEADS = D_MODEL // HEAD_DIM
C_LEFT_CHUNKS = 8
MAX_REL = 256
REL_TABLE = MAX_REL + CHUNK

LANES = 128
LOG2E = math.log2(math.e)
NEG = -1e30
VMEM_LIMIT = 56 * 1024 * 1024

FFN_TM = 512
FFN_FC = 256
PROJ_TM = 512
PROJ_CC = 512
DIFF_TQ = 256
DIFF_HEADS = 2
DSA_TQ = 512
BAND_TQ = 256
BAND_HEADS = 4

BF16 = jnp.bfloat16
F32 = jnp.float32


def _dot(a, b):
    return jnp.dot(a, b, preferred_element_type=F32)


def _dot_nt(a, b):
    return lax.dot_general(a, b, (((1,), (1,)), ((), ())), preferred_element_type=F32)


def _rms(xf, g):
    ms = jnp.mean(xf * xf, axis=-1, keepdims=True)
    return xf * lax.rsqrt(ms + EPS) * g


def _params(*sem):
    return pltpu.CompilerParams(dimension_semantics=sem, vmem_limit_bytes=VMEM_LIMIT)


def _ffn_kernel(*refs, n_mix, n_chunks):
    x_ref = refs[0]
    mix_refs = refs[1:1 + n_mix]
    wout_refs = refs[1 + n_mix:1 + 2 * n_mix]
    g_ref, wg_ref, wu_ref, wd_ref, out_ref = refs[1 + 2 * n_mix:]

    x = x_ref[...]
    if n_mix:
        m = _dot(mix_refs[0][...], wout_refs[0][...])
        for a_ref, w_ref in zip(mix_refs[1:], wout_refs[1:]):
            m = m + _dot(a_ref[...], w_ref[...])
        x = x + _rms(m, g_ref[0:1, :])
    h = _rms(x, g_ref[1:2, :]).astype(BF16)
    acc = jnp.zeros(x.shape, F32)
    for c in range(n_chunks):
        cols = slice(c * FFN_FC, (c + 1) * FFN_FC)
        a = _dot(h, wg_ref[:, cols])
        b = _dot(h, wu_ref[:, cols])
        t = (a / (1.0 + jnp.exp(-a))) * b
        acc = acc + _dot(t.astype(BF16), wd_ref[cols, :])
    out_ref[...] = x + 0.5 * _rms(acc, g_ref[2:3, :])


def _ffn_call(x2d, mixes, wouts, g3, wg, wu, wd, which):
    n, d = x2d.shape
    nc = D_FF // FFN_FC
    n_mix = len(mixes)
    row = lambda i: (i, 0)
    const2 = lambda i: (0, 0)
    in_specs = [pl.BlockSpec((FFN_TM, d), row)]
    in_specs += [pl.BlockSpec((FFN_TM, m.shape[1]), row) for m in mixes]
    in_specs += [pl.BlockSpec(w.shape, const2) for w in wouts]
    in_specs += [
        pl.BlockSpec(g3.shape, const2),
        pl.BlockSpec((None, None) + wg.shape[2:], lambda i: which + (0, 0)),
        pl.BlockSpec((None, None) + wu.shape[2:], lambda i: which + (0, 0)),
        pl.BlockSpec((None, None) + wd.shape[2:], lambda i: which + (0, 0)),
    ]
    return pl.pallas_call(
        functools.partial(_ffn_kernel, n_mix=n_mix, n_chunks=nc),
        grid=(n // FFN_TM,),
        in_specs=in_specs,
        out_specs=pl.BlockSpec((FFN_TM, d), row),
        out_shape=jax.ShapeDtypeStruct((n, d), F32),
        compiler_params=_params("parallel"),
        name="ffn_mix" if n_mix else "ffn",
    )(x2d, *mixes, *wouts, g3, wg, wu, wd)


def _rope(p, cos_t, sin_t, half):
    lane = lax.broadcasted_iota(jnp.int32, p.shape, 1)
    first = (lane & (2 * half - 1)) < half
    partner = jnp.where(first, pltpu.roll(p, LANES - half, 1), pltpu.roll(p, half, 1))
    return p * cos_t + partner * sin_t


def _proj_even_kernel(x_ref, g_ref, w_ref, tab_ref, qk_ref, avt_ref, bvt_ref, idx_ref, iwt_ref):
    h = _rms(x_ref[...], g_ref[0:1, :]).astype(BF16)
    c64, s64, c32, s32 = tab_ref[0], tab_ref[1], tab_ref[2], tab_ref[3]
    q_scale = HEAD_DIM ** -0.5 * LOG2E
    out_col = 0
    for c in (0, 1, 3, 4):
        p = _dot(h, w_ref[:, c * PROJ_CC:(c + 1) * PROJ_CC])
        for j in range(PROJ_CC // LANES):
            pj = _rope(p[:, j * LANES:(j + 1) * LANES], c64, s64, HEAD_DIM // 2)
            if c in (0, 3):
                pj = pj * q_scale
            qk_ref[:, out_col:out_col + LANES] = pj.astype(BF16)
            out_col += LANES
    avt_ref[...] = _dot(h, w_ref[:, 2 * PROJ_CC:3 * PROJ_CC]).T.astype(BF16)
    bvt_ref[0] = _dot(h, w_ref[:, 5 * PROJ_CC:6 * PROJ_CC]).T.astype(BF16)
    p = _dot(h, w_ref[:, 6 * PROJ_CC:7 * PROJ_CC])
    for j in range(3):
        pj = _rope(p[:, j * LANES:(j + 1) * LANES], c32, s32, IDX_DIM // 2)
        idx_ref[:, j * LANES:(j + 1) * LANES] = pj.astype(BF16)
    iw = p[:, 3 * LANES:4 * LANES] * (IDX_HEADS ** -0.5 * IDX_DIM ** -0.5)
    iwt_ref[...] = iw.T[0:IDX_HEADS, :]


def _proj_even_call(x2d, g, w, tabs, seq):
    n, d = x2d.shape
    steps_per_seq = seq // PROJ_TM
    return pl.pallas_call(
        _proj_even_kernel,
        grid=(n // PROJ_TM,),
        in_specs=[
            pl.BlockSpec((PROJ_TM, d), lambda i: (i, 0)),
            pl.BlockSpec(g.shape, lambda i: (0, 0)),
            pl.BlockSpec(w.shape, lambda i: (0, 0)),
            pl.BlockSpec((4, PROJ_TM, LANES), lambda i: (0, i % steps_per_seq, 0)),
        ],
        out_specs=[
            pl.BlockSpec((PROJ_TM, 4 * PROJ_CC), lambda i: (i, 0)),
            pl.BlockSpec((PROJ_CC, PROJ_TM), lambda i: (0, i)),
            pl.BlockSpec((1, PROJ_CC, PROJ_TM), lambda i: (i, 0, 0)),
            pl.BlockSpec((PROJ_TM, 3 * LANES), lambda i: (i, 0)),
            pl.BlockSpec((IDX_HEADS, PROJ_TM), lambda i: (0, i)),
        ],
        out_shape=[
            jax.ShapeDtypeStruct((n, 4 * PROJ_CC), BF16),
            jax.ShapeDtypeStruct((PROJ_CC, n), BF16),
            jax.ShapeDtypeStruct((n // PROJ_TM, PROJ_CC, PROJ_TM), BF16),
            jax.ShapeDtypeStruct((n, 3 * LANES), BF16),
            jax.ShapeDtypeStruct((IDX_HEADS, n), F32),
        ],
        compiler_params=_params("parallel"),
        name="proj_even",
    )(x2d, g, w, tabs)


def _proj_odd_kernel(x_ref, g_ref, w_ref, qk_ref, vt_ref):
    h = _rms(x_ref[...], g_ref[0:1, :]).astype(BF16)
    q_scale = HEAD_DIM ** -0.5 * LOG2E
    for c in range(2 * D_MODEL // PROJ_CC):
        p = _dot(h, w_ref[:, c * PROJ_CC:(c + 1) * PROJ_CC])
        if c * PROJ_CC < D_MODEL:
            p = p * q_scale
        qk_ref[:, c * PROJ_CC:(c + 1) * PROJ_CC] = p.astype(BF16)
    for c in range(D_MODEL // PROJ_CC):
        col = 2 * D_MODEL + c * PROJ_CC
        vt = _dot(h, w_ref[:, col:col + PROJ_CC]).T.astype(BF16)
        for j in range(vt_ref.shape[0]):
            vt_ref[j, c * PROJ_CC:(c + 1) * PROJ_CC, :] = vt[:, j * BAND_TQ:(j + 1) * BAND_TQ]


def _proj_odd_call(x2d, g, w):
    n, d = x2d.shape
    return pl.pallas_call(
        _proj_odd_kernel,
        grid=(n // PROJ_TM,),
        in_specs=[
            pl.BlockSpec((PROJ_TM, d), lambda i: (i, 0)),
            pl.BlockSpec(g.shape, lambda i: (0, 0)),
            pl.BlockSpec(w.shape, lambda i: (0, 0)),
        ],
        out_specs=[
            pl.BlockSpec((PROJ_TM, 2 * d), lambda i: (i, 0)),
            pl.BlockSpec((PROJ_TM // BAND_TQ, d, BAND_TQ), lambda i: (i, 0, 0)),
        ],
        out_shape=[
            jax.ShapeDtypeStruct((n, 2 * d), BF16),
            jax.ShapeDtypeStruct((n // BAND_TQ, d, BAND_TQ), BF16),
        ],
        compiler_params=_params("parallel"),
        name="proj_odd",
    )(x2d, g, w)


def _diff_attn_tile(c, tq, q_ref, k_ref, vt_ref, lam, sub_ref, o_ref, lambda_init):
    n_heads = q_ref.shape[1] // LANES
    left = c * tq
    lane = lax.broadcasted_iota(jnp.int32, (tq, LANES), 1)
    key_chunk = lax.broadcasted_iota(jnp.int32, (tq, tq), 0) >> 6
    qry_chunk = lax.broadcasted_iota(jnp.int32, (tq, tq), 1) >> 6
    diag_mask = key_chunk <= qry_chunk
    raw = []
    for h in range(n_heads):
        sl = slice(h * LANES, (h + 1) * LANES)
        q = q_ref[left:left + tq, sl]
        for keep in (lane < HEAD_DIM, lane >= HEAD_DIM):
            qc = jnp.where(keep, q, jnp.zeros_like(q))
            parts = [jnp.where(diag_mask, _dot_nt(k_ref[left:left + tq, sl], qc), NEG)]
            if c:
                parts.append(_dot_nt(k_ref[0:left, sl], qc))
            raw.append(parts)
    comps = []
    for parts in raw:
        m = functools.reduce(jnp.maximum, [jnp.max(s, axis=0, keepdims=True) for s in parts])
        parts = [jnp.exp2(s - m) for s in parts]
        l = functools.reduce(jnp.add, [jnp.sum(p, axis=0, keepdims=True) for p in parts])
        comps.append((parts, l))
    for h in range(n_heads):
        rows = slice(h * A_VDIM, (h + 1) * A_VDIM)
        outs = []
        for parts, l in (comps[2 * h], comps[2 * h + 1]):
            acc = _dot(vt_ref[rows, left:left + tq], parts[0].astype(BF16))
            if c:
                acc = acc + _dot(vt_ref[rows, 0:left], parts[1].astype(BF16))
            outs.append(acc / l)
        o_t = outs[0] - lam * outs[1]
        ms = jnp.mean(o_t * o_t, axis=0, keepdims=True)
        o_t = o_t * lax.rsqrt(ms + EPS) * sub_ref[...] * (1.0 - lambda_init)
        o_ref[left:left + tq, h * LANES:(h + 1) * LANES] = o_t.T.astype(o_ref.dtype)


def _diff_attn_kernel(q_ref, k_ref, vt_ref, lam_ref, sub_ref, o_ref, *, lambda_init, tq):
    lp = lam_ref[...]
    lam = (jnp.exp(jnp.sum(lp[0:1] * lp[1:2], axis=-1, keepdims=True))
           - jnp.exp(jnp.sum(lp[2:3] * lp[3:4], axis=-1, keepdims=True)) + lambda_init)
    for c in range(q_ref.shape[0] // tq):
        _diff_attn_tile(c, tq, q_ref, k_ref, vt_ref, lam, sub_ref, o_ref, lambda_init)


def _diff_attn_call(qk, avt, lam_p, subln, batch, seq, lambda_init):
    n = qk.shape[0]
    n_groups = A_HEADS // DIFF_HEADS
    width = DIFF_HEADS * A_VDIM
    sub_t = jnp.broadcast_to(subln.astype(F32)[:, None], (A_VDIM, DIFF_TQ))
    return pl.pallas_call(
        functools.partial(_diff_attn_kernel, lambda_init=lambda_init, tq=DIFF_TQ),
        grid=(batch, n_groups),
        in_specs=[
            pl.BlockSpec((seq, width), lambda b, g: (b, g)),
            pl.BlockSpec((seq, width), lambda b, g: (b, n_groups + g)),
            pl.BlockSpec((width, seq), lambda b, g: (g, b)),
            pl.BlockSpec(lam_p.shape, lambda b, g: (0, 0)),
            pl.BlockSpec(sub_t.shape, lambda b, g: (0, 0)),
        ],
        out_specs=pl.BlockSpec((seq, width), lambda b, g: (b, g)),
        out_shape=jax.ShapeDtypeStruct((n, A_HEADS * A_VDIM), BF16),
        compiler_params=_params("parallel", "parallel"),
        name="diff_attn",
    )(qk, qk, avt, lam_p, sub_t)


def _dsa_kernel(iq_ref, ik_ref, iwt_ref, q_ref, k_ref, vt_ref, o_ref,
                qh_ref, qm_ref, key_ref, hi_ref, lo_ref, bias_ref, acc_ref, *, topk):
    i = pl.program_id(1)
    tq = q_ref.shape[0]
    kb_size = key_ref.shape[1]
    n_kb = i + 1
    kf = float(topk)
    pack = 16
    rows16 = kb_size // pack
    min16 = jnp.int16(-32768)
    one16 = jnp.int16(1)
    zero16 = jnp.int16(0)

    lane = lax.broadcasted_iota(jnp.int32, (tq, LANES), 1)
    idx_per_vreg = LANES // IDX_DIM
    for h in range(IDX_HEADS):
        x = iq_ref[:, (h // idx_per_vreg) * LANES:(h // idx_per_vreg + 1) * LANES]
        qh_ref[h] = jnp.where((lane >> 5) == (h % idx_per_vreg), x, jnp.zeros_like(x))
    for h in range(B_HEADS):
        x = q_ref[:, (h // 2) * LANES:(h // 2 + 1) * LANES]
        qm_ref[h] = jnp.where((lane >> 6) == (h % 2), x, jnp.zeros_like(x))

    q_chunk = (i * tq + lax.broadcasted_iota(jnp.int32, (kb_size, tq), 1)) >> 6

    def visible(kb):
        k_chunk = (kb * kb_size + lax.broadcasted_iota(jnp.int32, (kb_size, tq), 0)) >> 6
        return k_chunk <= q_chunk

    def score_block(kb, carry):
        start = pl.multiple_of(kb * kb_size, kb_size)
        ikb = ik_ref[pl.ds(start, kb_size), :]
        dots = [_dot_nt(ikb, qh_ref[h]) for h in range(IDX_HEADS)]
        score = jnp.maximum(dots[0], 0.0) * iwt_ref[0:1, :]
        for h in range(1, IDX_HEADS):
            score = score + jnp.maximum(dots[h], 0.0) * iwt_ref[h:h + 1, :]
        score = jnp.where(visible(kb), score + 0.0, -jnp.inf)
        bits = pltpu.bitcast(score, jnp.int32)
        key = bits ^ ((bits >> 31) & 0x7FFFFFFF)
        key_ref[kb] = key
        hi_ref[kb] = (key >> 16).astype(jnp.int16)
        lo_ref[kb] = ((key & 0xFFFF) - 32768).astype(jnp.int16)
        return carry

    lax.fori_loop(0, n_kb, score_block, 0)

    def tile16(x):
        return jnp.broadcast_to(x, (pack, tq)).astype(jnp.int16)

    def count16(ref, pred):
        def block(kb, acc):
            for r in range(rows16):
                x = ref[kb, r * pack:(r + 1) * pack, :]
                acc = acc + jnp.where(pred(x), one16, zero16)
            return acc
        acc = lax.fori_loop(0, n_kb, block, jnp.zeros((pack, tq), jnp.int16))
        return jnp.sum(acc.astype(F32), axis=0, keepdims=True)

    def kth_largest16(ref, target):
        def step(it, t):
            cand = t + lax.shift_left(jnp.int32(1), 15 - it)
            c16 = tile16(cand)
            cnt = count16(ref, lambda x: x >= c16)
            return jnp.where(cnt >= target, cand, t)
        return lax.fori_loop(0, 16, step, jnp.full((1, tq), -32768, jnp.int32))

    t_hi = kth_largest16(hi_ref, kf)
    t_hi16 = tile16(t_hi)

    def restrict_low(kb, acc):
        for r in range(rows16):
            sl = slice(r * pack, (r + 1) * pack)
            h16 = hi_ref[kb, sl, :]
            acc = acc + jnp.where(h16 > t_hi16, one16, zero16)
            lo_ref[kb, sl, :] = jnp.where(h16 == t_hi16, lo_ref[kb, sl, :], min16)
        return acc

    above = lax.fori_loop(0, n_kb, restrict_low, jnp.zeros((pack, tq), jnp.int16))
    want_low = kf - jnp.sum(above.astype(F32), axis=0, keepdims=True)
    t_lo = kth_largest16(lo_ref, want_low)
    t_lo16 = tile16(t_lo)
    need = want_low - count16(lo_ref, lambda x: x > t_lo16)
    thr = lax.shift_left(t_hi, 16) | ((t_lo + 32768) & 0xFFFF)

    r_i = lax.broadcasted_iota(jnp.int32, (LANES, LANES), 0)
    c_i = lax.broadcasted_iota(jnp.int32, (LANES, LANES), 1)
    strict_lower = jnp.where(c_i < r_i, 1.0, 0.0).astype(BF16)

    def select_block(kb, offset):
        vis = visible(kb)
        for j in range(kb_size // LANES):
            sl = slice(j * LANES, (j + 1) * LANES)
            kj = key_ref[kb, sl, :]
            eq = kj == thr
            eq_f = jnp.where(eq, 1.0, 0.0)
            before = _dot(strict_lower, eq_f.astype(BF16)) + offset
            sel = ((kj > thr) | (eq & (before < need))) & vis[sl]
            bias_ref[kb, sl, :] = jnp.where(sel, 0.0, NEG)
            offset = offset + jnp.sum(eq_f, axis=0, keepdims=True)
        return offset

    lax.fori_loop(0, n_kb, select_block, jnp.zeros((1, tq), F32))

    acc_ref[...] = jnp.zeros(acc_ref.shape, F32)

    def attn_block(kb, carry):
        ms, ls = carry
        start = pl.multiple_of(kb * kb_size, kb_size)
        bias = bias_ref[kb]
        new_m, new_l = [], []
        scores = []
        for h in range(B_HEADS):
            k2 = k_ref[pl.ds(start, kb_size), (h // 2) * LANES:(h // 2 + 1) * LANES]
            s = _dot_nt(k2, qm_ref[h]) + bias
            scores.append(s)
            new_m.append(jnp.maximum(ms[h], jnp.max(s, axis=0, keepdims=True)))
        for h in range(B_HEADS):
            alpha = jnp.exp2(ms[h] - new_m[h])
            p = jnp.exp2(scores[h] - new_m[h])
            new_l.append(alpha * ls[h] + jnp.sum(p, axis=0, keepdims=True))
            pv = _dot(vt_ref[kb, h * HEAD_DIM:(h + 1) * HEAD_DIM, :], p.astype(BF16))
            acc_ref[h] = alpha * acc_ref[h] + pv
        return tuple(new_m), tuple(new_l)

    m0 = tuple(jnp.full((1, tq), NEG, F32) for _ in range(B_HEADS))
    l0 = tuple(jnp.zeros((1, tq), F32) for _ in range(B_HEADS))
    _, ls = lax.fori_loop(0, n_kb, attn_block, (m0, l0))
    o_t = jnp.concatenate([acc_ref[h] / ls[h] for h in range(B_HEADS)], axis=0)
    o_ref[...] = o_t.T.astype(o_ref.dtype)


def _dsa_call(qk, vt, idx, iwt, batch, seq, topk):
    n = qk.shape[0]
    nq = seq // DSA_TQ
    width = B_HEADS * HEAD_DIM
    assert vt.shape[2] == DSA_TQ, "key blocks must match the projection's token tile"
    return pl.pallas_call(
        functools.partial(_dsa_kernel, topk=topk),
        grid=(batch, nq),
        in_specs=[
            pl.BlockSpec((DSA_TQ, 2 * LANES), lambda b, i: (b * nq + i, 0)),
            pl.BlockSpec((seq, LANES), lambda b, i: (b, 2)),
            pl.BlockSpec((IDX_HEADS, DSA_TQ), lambda b, i: (0, b * nq + i)),
            pl.BlockSpec((DSA_TQ, width), lambda b, i: (b * nq + i, 2)),
            pl.BlockSpec((seq, width), lambda b, i: (b, 3)),
            pl.BlockSpec((nq, width, DSA_TQ), lambda b, i: (b, 0, 0)),
        ],
        out_specs=pl.BlockSpec((DSA_TQ, width), lambda b, i: (b * nq + i, 0)),
        out_shape=jax.ShapeDtypeStruct((n, width), BF16),
        scratch_shapes=[
            pltpu.VMEM((IDX_HEADS, DSA_TQ, LANES), BF16),
            pltpu.VMEM((B_HEADS, DSA_TQ, LANES), BF16),
            pltpu.VMEM((nq, DSA_TQ, DSA_TQ), jnp.int32),
            pltpu.VMEM((nq, DSA_TQ, DSA_TQ), jnp.int16),
            pltpu.VMEM((nq, DSA_TQ, DSA_TQ), jnp.int16),
            pltpu.VMEM((nq, DSA_TQ, DSA_TQ), F32),
            pltpu.VMEM((B_HEADS, HEAD_DIM, DSA_TQ), F32),
        ],
        compiler_params=_params("parallel", "arbitrary"),
        name="dsa",
    )(idx, idx, iwt, qk, qk, vt)


def _band_attn_kernel(q_ref, k_ref, vt_ref, tbl_ref, o_ref):
    n_heads = tbl_ref.shape[0]
    tq = tbl_ref.shape[2]
    n_blk = tbl_ref.shape[1] // tq
    n_tiles = q_ref.shape[0] // tq
    lane = lax.broadcasted_iota(jnp.int32, (tq, LANES), 1)
    for c in range(n_tiles):
        blocks = [(d, c - (n_blk - 1) + d) for d in range(n_blk) if c - (n_blk - 1) + d >= 0]
        scores, ms = [], []
        for h in range(n_heads):
            sl = slice((h // 2) * LANES, (h // 2 + 1) * LANES)
            q2 = q_ref[c * tq:(c + 1) * tq, sl]
            qm = jnp.where((lane >> 6) == (h % 2), q2, jnp.zeros_like(q2))
            parts = [_dot_nt(k_ref[kb * tq:(kb + 1) * tq, sl], qm) + tbl_ref[h, d * tq:(d + 1) * tq, :]
                     for d, kb in blocks]
            scores.append(parts)
            ms.append(functools.reduce(jnp.maximum, [jnp.max(s, axis=0, keepdims=True) for s in parts]))
        outs = []
        for h in range(n_heads):
            ps = [jnp.exp2(s - ms[h]) for s in scores[h]]
            l = functools.reduce(jnp.add, [jnp.sum(p, axis=0, keepdims=True) for p in ps])
            rows = slice(h * HEAD_DIM, (h + 1) * HEAD_DIM)
            pv = functools.reduce(jnp.add, [_dot(vt_ref[kb, rows, :], p.astype(BF16))
                                            for (d, kb), p in zip(blocks, ps)])
            outs.append(pv / l)
        o_ref[c * tq:(c + 1) * tq, :] = jnp.concatenate(outs, axis=0).T.astype(o_ref.dtype)


def _band_attn_call(qk, vt, tbl, batch, seq):
    n = qk.shape[0]
    nq = seq // BAND_TQ
    n_groups = C_HEADS // BAND_HEADS
    width = BAND_HEADS * HEAD_DIM
    return pl.pallas_call(
        _band_attn_kernel,
        grid=(n_groups, batch),
        in_specs=[
            pl.BlockSpec((seq, width), lambda g, b: (b, g)),
            pl.BlockSpec((seq, width), lambda g, b: (b, n_groups + g)),
            pl.BlockSpec((nq, width, BAND_TQ), lambda g, b: (b, g, 0)),
            pl.BlockSpec((BAND_HEADS,) + tbl.shape[1:], lambda g, b: (g, 0, 0)),
        ],
        out_specs=pl.BlockSpec((seq, width), lambda g, b: (b, g)),
        out_shape=jax.ShapeDtypeStruct((n, D_MODEL), BF16),
        compiler_params=_params("parallel", "parallel"),
        name="band_attn",
    )(qk, qk, vt, tbl)


def _rope_tables(seq):
    def one(dim):
        inv = ROPE_THETA ** (-jnp.arange(0, dim, 2, dtype=F32) / dim)
        ang = jnp.arange(seq, dtype=F32)[:, None] * inv[None, :]
        cos, sin = jnp.cos(ang), jnp.sin(ang)
        reps = LANES // dim
        return (jnp.tile(jnp.concatenate([cos, cos], axis=-1), (1, reps)),
                jnp.tile(jnp.concatenate([-sin, sin], axis=-1), (1, reps)))
    c64, s64 = one(HEAD_DIM)
    c32, s32 = one(IDX_DIM)
    return jnp.stack([c64, s64, c32, s32])


def _even_w_in_layout(w):
    d = w.shape[0]
    n_qkv = 6 * PROJ_CC
    iq = w[:, n_qkv:n_qkv + IDX_HEADS * IDX_DIM]
    ik = w[:, n_qkv + IDX_HEADS * IDX_DIM:n_qkv + IDX_HEADS * IDX_DIM + IDX_DIM]
    iw = w[:, n_qkv + IDX_HEADS * IDX_DIM + IDX_DIM:]
    return jnp.concatenate(
        [w[:, :n_qkv], iq, jnp.tile(ik, (1, LANES // IDX_DIM)),
         iw, jnp.zeros((d, LANES - IDX_HEADS), w.dtype)], axis=1).astype(BF16)


def _band_table_kernel(g_ref, o_ref):
    window, tq = o_ref.shape[1], o_ref.shape[2]
    span = g_ref.shape[2]
    gb = jnp.broadcast_to(g_ref[0], (window, span))
    tbl = pltpu.roll(gb, span - window, 1, stride=1, stride_axis=0)[:, :tq]
    kc = lax.broadcasted_iota(jnp.int32, (window, tq), 0) >> 6
    qc = (window - tq + lax.broadcasted_iota(jnp.int32, (window, tq), 1)) >> 6
    in_band = (kc <= qc) & (kc >= qc - C_LEFT_CHUNKS)
    o_ref[0] = jnp.where(in_band, tbl * LOG2E, NEG)


def _band_bias_table(rel_bias):
    h = rel_bias.shape[0]
    left = C_LEFT_CHUNKS * CHUNK
    window = left + BAND_TQ
    span = BAND_TQ + window
    n_near = window - left + 1 - CHUNK
    n_far = span - n_near - REL_TABLE
    rel = rel_bias.astype(F32)
    g = jnp.concatenate([jnp.broadcast_to(rel[:, :1], (h, n_near)), rel,
                         jnp.broadcast_to(rel[:, REL_TABLE - 1:], (h, n_far))], axis=1)
    return pl.pallas_call(
        _band_table_kernel,
        grid=(h,),
        in_specs=[pl.BlockSpec((1, 1, span), lambda i: (i, 0, 0))],
        out_specs=pl.BlockSpec((1, window, BAND_TQ), lambda i: (i, 0, 0)),
        out_shape=jax.ShapeDtypeStruct((h, window, BAND_TQ), F32),
        compiler_params=_params("parallel"),
        name="band_table",
    )(g[:, None, :])


def kernel(x, norm_g, ffn_wg, ffn_wu, ffn_wd, even_w_in, even_w_out, even_lambda, even_subln,
           odd_w_in, odd_w_out, odd_rel_bias):
    batch, seq, d = x.shape
    depth = norm_g.shape[0]
    topk = min(TOPK_MAX, seq // 4)
    x2d = x.reshape(batch * seq, d)
    tabs = _rope_tables(seq)
    ffn_w = (ffn_wg.astype(BF16), ffn_wu.astype(BF16), ffn_wd.astype(BF16))
    pending = ((), ())
    g_mix = norm_g[0, 3]
    for l in range(depth):
        g = norm_g[l]
        x2d = _ffn_call(x2d, pending[0], pending[1], jnp.stack([g_mix, g[0], g[1]]), *ffn_w, (l, 0))
        if l % 2 == 0:
            e = l // 2
            lambda_init = 0.8 - 0.6 * math.exp(-0.3 * l)
            qk, avt, bvt, idx, iwt = _proj_even_call(x2d, g[2:3], _even_w_in_layout(even_w_in[e]), tabs, seq)
            o_a = _diff_attn_call(qk, avt, even_lambda[e], even_subln[e], batch, seq, lambda_init)
            o_b = _dsa_call(qk, bvt, idx, iwt, batch, seq, topk)
            w_out = even_w_out[e].astype(BF16)
            split = A_HEADS * A_VDIM
            pending = ((o_a, o_b), (w_out[:split], w_out[split:]))
        else:
            o = l // 2
            qk, vt = _proj_odd_call(x2d, g[2:3], odd_w_in[o].astype(BF16))
            o_c = _band_attn_call(qk, vt, _band_bias_table(odd_rel_bias[o]), batch, seq)
            pending = ((o_c,), (odd_w_out[o].astype(BF16),))
        g_mix = g[3]
        x2d = _ffn_call(x2d, pending[0], pending[1], jnp.stack([g_mix, g[4], g[5]]), *ffn_w, (l, 1))
        pending = ((), ())
    return x2d.reshape(batch, seq, d)
```

```python
import functools
import math

import jax
import jax.numpy as jnp
from jax import lax
from jax.experimental import pallas as pl
from jax.experimental.pallas import tpu as pltpu

D_MODEL = 1024
CHUNK = 64
HEAD_DIM = 64
ROPE_THETA = 10000.0
EPS = 1e-6
D_FF = 2816
A_HEADS = D_MODEL // (4 * HEAD_DIM)
A_VDIM = 2 * HEAD_DIM
B_HEADS = D_MODEL // (2 * HEAD_DIM)
IDX_HEADS = 8
IDX_DIM = 32
TOPK_MAX = 256
C_HEADS = D_MODEL // HEAD_DIM
C_LEFT_CHUNKS = 8
MAX_REL = 256
REL_TABLE = MAX_REL + CHUNK

LANES = 128
LOG2E = math.log2(math.e)
NEG = -1e30
VMEM_LIMIT = 56 * 1024 * 1024

FFN_TM = 512
FFN_FC = 256
PROJ_TM = 512
PROJ_CC = 512
DIFF_TQ = 256
DIFF_HEADS = 2
DSA_TQ = 512
BAND_TQ = 256
BAND_HEADS = 4

BF16 = jnp.bfloat16
F32 = jnp.float32


def _dot(a, b):
    return jnp.dot(a, b, preferred_element_type=F32)


def _dot_nt(a, b):
    return lax.dot_general(a, b, (((1,), (1,)), ((), ())), preferred_element_type=F32)


def _rms(xf, g):
    ms = jnp.mean(xf * xf, axis=-1, keepdims=True)
    return xf * lax.rsqrt(ms + EPS) * g


def _params(*sem):
    return pltpu.CompilerParams(dimension_semantics=sem, vmem_limit_bytes=VMEM_LIMIT)


def _ffn_kernel(*refs, n_mix, n_chunks):
    x_ref = refs[0]
    mix_refs = refs[1:1 + n_mix]
    wout_refs = refs[1 + n_mix:1 + 2 * n_mix]
    g_ref, wg_ref, wu_ref, wd_ref, out_ref = refs[1 + 2 * n_mix:]

    x = x_ref[...]
    if n_mix:
        m = _dot(mix_refs[0][...], wout_refs[0][...])
        for a_ref, w_ref in zip(mix_refs[1:], wout_refs[1:]):
            m = m + _dot(a_ref[...], w_ref[...])
        x = x + _rms(m, g_ref[0:1, :])
    h = _rms(x, g_ref[1:2, :]).astype(BF16)
    acc = jnp.zeros(x.shape, F32)
    for c in range(n_chunks):
        cols = slice(c * FFN_FC, (c + 1) * FFN_FC)
        a = _dot(h, wg_ref[:, cols])
        b = _dot(h, wu_ref[:, cols])
        t = (a / (1.0 + jnp.exp(-a))) * b
        acc = acc + _dot(t.astype(BF16), wd_ref[cols, :])
    out_ref[...] = x + 0.5 * _rms(acc, g_ref[2:3, :])


def _ffn_call(x2d, mixes, wouts, g3, wg, wu, wd, which):
    n, d = x2d.shape
    nc = D_FF // FFN_FC
    n_mix = len(mixes)
    row = lambda i: (i, 0)
    const2 = lambda i: (0, 0)
    in_specs = [pl.BlockSpec((FFN_TM, d), row)]
    in_specs += [pl.BlockSpec((FFN_TM, m.shape[1]), row) for m in mixes]
    in_specs += [pl.BlockSpec(w.shape, const2) for w in wouts]
    in_specs += [
        pl.BlockSpec(g3.shape, const2),
        pl.BlockSpec((None, None) + wg.shape[2:], lambda i: which + (0, 0)),
        pl.BlockSpec((None, None) + wu.shape[2:], lambda i: which + (0, 0)),
        pl.BlockSpec((None, None) + wd.shape[2:], lambda i: which + (0, 0)),
    ]
    return pl.pallas_call(
        functools.partial(_ffn_kernel, n_mix=n_mix, n_chunks=nc),
        grid=(n // FFN_TM,),
        in_specs=in_specs,
        out_specs=pl.BlockSpec((FFN_TM, d), row),
        out_shape=jax.ShapeDtypeStruct((n, d), F32),
        compiler_params=_params("parallel"),
        name="ffn_mix" if n_mix else "ffn",
    )(x2d, *mixes, *wouts, g3, wg, wu, wd)


def _rope(p, cos_t, sin_t, half):
    lane = lax.broadcasted_iota(jnp.int32, p.shape, 1)
    first = (lane & (2 * half - 1)) < half
    partner = jnp.where(first, pltpu.roll(p, LANES - half, 1), pltpu.roll(p, half, 1))
    return p * cos_t + partner * sin_t


def _proj_even_kernel(x_ref, g_ref, w_ref, tab_ref, qk_ref, avt_ref, bvt_ref, idx_ref, iwt_ref):
    h = _rms(x_ref[...], g_ref[0:1, :]).astype(BF16)
    c64, s64, c32, s32 = tab_ref[0], tab_ref[1], tab_ref[2], tab_ref[3]
    q_scale = HEAD_DIM ** -0.5 * LOG2E
    out_col = 0
    for c in (0, 1, 3, 4):
        p = _dot(h, w_ref[:, c * PROJ_CC:(c + 1) * PROJ_CC])
        for j in range(PROJ_CC // LANES):
            pj = _rope(p[:, j * LANES:(j + 1) * LANES], c64, s64, HEAD_DIM // 2)
            if c in (0, 3):
                pj = pj * q_scale
            qk_ref[:, out_col:out_col + LANES] = pj.astype(BF16)
            out_col += LANES
    avt_ref[...] = _dot(h, w_ref[:, 2 * PROJ_CC:3 * PROJ_CC]).T.astype(BF16)
    bvt_ref[0] = _dot(h, w_ref[:, 5 * PROJ_CC:6 * PROJ_CC]).T.astype(BF16)
    p = _dot(h, w_ref[:, 6 * PROJ_CC:7 * PROJ_CC])
    for j in range(3):
        pj = _rope(p[:, j * LANES:(j + 1) * LANES], c32, s32, IDX_DIM // 2)
        idx_ref[:, j * LANES:(j + 1) * LANES] = pj.astype(BF16)
    iw = p[:, 3 * LANES:4 * LANES] * (IDX_HEADS ** -0.5 * IDX_DIM ** -0.5)
    iwt_ref[...] = iw.T[0:IDX_HEADS, :]


def _proj_even_call(x2d, g, w, tabs, seq):
    n, d = x2d.shape
    steps_per_seq = seq // PROJ_TM
    return pl.pallas_call(
        _proj_even_kernel,
        grid=(n // PROJ_TM,),
        in_specs=[
            pl.BlockSpec((PROJ_TM, d), lambda i: (i, 0)),
            pl.BlockSpec(g.shape, lambda i: (0, 0)),
            pl.BlockSpec(w.shape, lambda i: (0, 0)),
            pl.BlockSpec((4, PROJ_TM, LANES), lambda i: (0, i % steps_per_seq, 0)),
        ],
        out_specs=[
            pl.BlockSpec((PROJ_TM, 4 * PROJ_CC), lambda i: (i, 0)),
            pl.BlockSpec((PROJ_CC, PROJ_TM), lambda i: (0, i)),
            pl.BlockSpec((1, PROJ_CC, PROJ_TM), lambda i: (i, 0, 0)),
            pl.BlockSpec((PROJ_TM, 3 * LANES), lambda i: (i, 0)),
            pl.BlockSpec((IDX_HEADS, PROJ_TM), lambda i: (0, i)),
        ],
        out_shape=[
            jax.ShapeDtypeStruct((n, 4 * PROJ_CC), BF16),
            jax.ShapeDtypeStruct((PROJ_CC, n), BF16),
            jax.ShapeDtypeStruct((n // PROJ_TM, PROJ_CC, PROJ_TM), BF16),
            jax.ShapeDtypeStruct((n, 3 * LANES), BF16),
            jax.ShapeDtypeStruct((IDX_HEADS, n), F32),
        ],
        compiler_params=_params("parallel"),
        name="proj_even",
    )(x2d, g, w, tabs)


def _proj_odd_kernel(x_ref, g_ref, w_ref, qk_ref, vt_ref):
    h = _rms(x_ref[...], g_ref[0:1, :]).astype(BF16)
    q_scale = HEAD_DIM ** -0.5 * LOG2E
    for c in range(2 * D_MODEL // PROJ_CC):
        p = _dot(h, w_ref[:, c * PROJ_CC:(c + 1) * PROJ_CC])
        if c * PROJ_CC < D_MODEL:
            p = p * q_scale
        qk_ref[:, c * PROJ_CC:(c + 1) * PROJ_CC] = p.astype(BF16)
    for c in range(D_MODEL // PROJ_CC):
        col = 2 * D_MODEL + c * PROJ_CC
        vt = _dot(h, w_ref[:, col:col + PROJ_CC]).T.astype(BF16)
        for j in range(vt_ref.shape[0]):
            vt_ref[j, c * PROJ_CC:(c + 1) * PROJ_CC, :] = vt[:, j * BAND_TQ:(j + 1) * BAND_TQ]


def _proj_odd_call(x2d, g, w):
    n, d = x2d.shape
    return pl.pallas_call(
        _proj_odd_kernel,
        grid=(n // PROJ_TM,),
        in_specs=[
            pl.BlockSpec((PROJ_TM, d), lambda i: (i, 0)),
            pl.BlockSpec(g.shape, lambda i: (0, 0)),
            pl.BlockSpec(w.shape, lambda i: (0, 0)),
        ],
        out_specs=[
            pl.BlockSpec((PROJ_TM, 2 * d), lambda i: (i, 0)),
            pl.BlockSpec((PROJ_TM // BAND_TQ, d, BAND_TQ), lambda i: (i, 0, 0)),
        ],
        out_shape=[
            jax.ShapeDtypeStruct((n, 2 * d), BF16),
            jax.ShapeDtypeStruct((n // BAND_TQ, d, BAND_TQ), BF16),
        ],
        compiler_params=_params("parallel"),
        name="proj_odd",
    )(x2d, g, w)


def _diff_attn_tile(c, tq, q_ref, k_ref, vt_ref, lam, sub_ref, o_ref, lambda_init):
    n_heads = q_ref.shape[1] // LANES
    left = c * tq
    lane = lax.broadcasted_iota(jnp.int32, (tq, LANES), 1)
    key_chunk = lax.broadcasted_iota(jnp.int32, (tq, tq), 0) >> 6
    qry_chunk = lax.broadcasted_iota(jnp.int32, (tq, tq), 1) >> 6
    diag_mask = key_chunk <= qry_chunk
    raw = []
    for h in range(n_heads):
        sl = slice(h * LANES, (h + 1) * LANES)
        q = q_ref[left:left + tq, sl]
        for keep in (lane < HEAD_DIM, lane >= HEAD_DIM):
            qc = jnp.where(keep, q, jnp.zeros_like(q))
            parts = [jnp.where(diag_mask, _dot_nt(k_ref[left:left + tq, sl], qc), NEG)]
            if c:
                parts.append(_dot_nt(k_ref[0:left, sl], qc))
            raw.append(parts)
    comps = []
    for parts in raw:
        m = functools.reduce(jnp.maximum, [jnp.max(s, axis=0, keepdims=True) for s in parts])
        parts = [jnp.exp2(s - m) for s in parts]
        l = functools.reduce(jnp.add, [jnp.sum(p, axis=0, keepdims=True) for p in parts])
        comps.append((parts, l))
    for h in range(n_heads):
        rows = slice(h * A_VDIM, (h + 1) * A_VDIM)
        outs = []
        for parts, l in (comps[2 * h], comps[2 * h + 1]):
            acc = _dot(vt_ref[rows, left:left + tq], parts[0].astype(BF16))
            if c:
                acc = acc + _dot(vt_ref[rows, 0:left], parts[1].astype(BF16))
            outs.append(acc / l)
        o_t = outs[0] - lam * outs[1]
        ms = jnp.mean(o_t * o_t, axis=0, keepdims=True)
        o_t = o_t * lax.rsqrt(ms + EPS) * sub_ref[...] * (1.0 - lambda_init)
        o_ref[left:left + tq, h * LANES:(h + 1) * LANES] = o_t.T.astype(o_ref.dtype)


def _diff_attn_kernel(q_ref, k_ref, vt_ref, lam_ref, sub_ref, o_ref, *, lambda_init, tq):
    lp = lam_ref[...]
    lam = (jnp.exp(jnp.sum(lp[0:1] * lp[1:2], axis=-1, keepdims=True))
           - jnp.exp(jnp.sum(lp[2:3] * lp[3:4], axis=-1, keepdims=True)) + lambda_init)
    for c in range(q_ref.shape[0] // tq):
        _diff_attn_tile(c, tq, q_ref, k_ref, vt_ref, lam, sub_ref, o_ref, lambda_init)


def _diff_attn_call(qk, avt, lam_p, subln, batch, seq, lambda_init):
    n = qk.shape[0]
    n_groups = A_HEADS // DIFF_HEADS
    width = DIFF_HEADS * A_VDIM
    sub_t = jnp.broadcast_to(subln.astype(F32)[:, None], (A_VDIM, DIFF_TQ))
    return pl.pallas_call(
        functools.partial(_diff_attn_kernel, lambda_init=lambda_init, tq=DIFF_TQ),
        grid=(batch, n_groups),
        in_specs=[
            pl.BlockSpec((seq, width), lambda b, g: (b, g)),
            pl.BlockSpec((seq, width), lambda b, g: (b, n_groups + g)),
            pl.BlockSpec((width, seq), lambda b, g: (g, b)),
            pl.BlockSpec(lam_p.shape, lambda b, g: (0, 0)),
            pl.BlockSpec(sub_t.shape, lambda b, g: (0, 0)),
        ],
        out_specs=pl.BlockSpec((seq, width), lambda b, g: (b, g)),
        out_shape=jax.ShapeDtypeStruct((n, A_HEADS * A_VDIM), BF16),
        compiler_params=_params("parallel", "parallel"),
        name="diff_attn",
    )(qk, qk, avt, lam_p, sub_t)


def _dsa_kernel(iq_ref, ik_ref, iwt_ref, q_ref, k_ref, vt_ref, o_ref,
                qh_ref, qm_ref, score_ref, bias_ref, acc_ref, *, topk):
    i = pl.program_id(1)
    tq = q_ref.shape[0]
    kb_size = score_ref.shape[1]
    n_kb = i + 1
    kf = float(topk)

    lane = lax.broadcasted_iota(jnp.int32, (tq, LANES), 1)
    idx_per_vreg = LANES // IDX_DIM
    for h in range(IDX_HEADS):
        x = iq_ref[:, (h // idx_per_vreg) * LANES:(h // idx_per_vreg + 1) * LANES]
        qh_ref[h] = jnp.where((lane >> 5) == (h % idx_per_vreg), x, jnp.zeros_like(x))
    for h in range(B_HEADS):
        x = q_ref[:, (h // 2) * LANES:(h // 2 + 1) * LANES]
        qm_ref[h] = jnp.where((lane >> 6) == (h % 2), x, jnp.zeros_like(x))

    q_chunk = (i * tq + lax.broadcasted_iota(jnp.int32, (kb_size, tq), 1)) >> 6

    def visible(kb):
        k_chunk = (kb * kb_size + lax.broadcasted_iota(jnp.int32, (kb_size, tq), 0)) >> 6
        return k_chunk <= q_chunk

    def score_block(kb, carry):
        start = pl.multiple_of(kb * kb_size, kb_size)
        ikb = ik_ref[pl.ds(start, kb_size), :]
        dots = [_dot_nt(ikb, qh_ref[h]) for h in range(IDX_HEADS)]
        score = jnp.maximum(dots[0], 0.0) * iwt_ref[0:1, :]
        for h in range(1, IDX_HEADS):
            score = score + jnp.maximum(dots[h], 0.0) * iwt_ref[h:h + 1, :]
        score_ref[kb] = jnp.where(visible(kb), score + 0.0, -jnp.inf)
        return carry

    lax.fori_loop(0, n_kb, score_block, 0)

    def count(pred):
        def block(kb, acc):
            for r in range(kb_size // 8):
                x = score_ref[kb, r * 8:(r + 1) * 8, :]
                acc = acc + jnp.where(pred(x), 1.0, 0.0)
            return acc
        acc = lax.fori_loop(0, n_kb, block, jnp.zeros((8, tq), F32))
        return jnp.sum(acc, axis=0, keepdims=True)

    def key_to_float(key):
        return pltpu.bitcast(key ^ ((key >> 31) & 0x7FFFFFFF), F32)

    neg_inf_key = jnp.int32(-2139095041)

    def step(it, t):
        cand = t + lax.shift_left(jnp.int32(1), 31 - it)
        c_f = key_to_float(cand)
        cnt = count(lambda x: x >= c_f)
        return jnp.where((cand > t) & (cnt >= kf), cand, t)

    thr_key = lax.fori_loop(0, 32, step, jnp.full((1, tq), neg_inf_key, jnp.int32))
    thr = key_to_float(thr_key)
    need = kf - count(lambda x: x > thr)

    r_i = lax.broadcasted_iota(jnp.int32, (LANES, LANES), 0)
    c_i = lax.broadcasted_iota(jnp.int32, (LANES, LANES), 1)
    strict_lower = jnp.where(c_i < r_i, 1.0, 0.0).astype(BF16)

    def select_block(kb, offset):
        vis = visible(kb)
        for j in range(kb_size // LANES):
            sl = slice(j * LANES, (j + 1) * LANES)
            kj = score_ref[kb, sl, :]
            eq = kj == thr
            eq_f = jnp.where(eq, 1.0, 0.0)
            before = _dot(strict_lower, eq_f.astype(BF16)) + offset
            sel = ((kj > thr) | (eq & (before < need))) & vis[sl]
            bias_ref[kb, sl, :] = jnp.where(sel, 0.0, NEG)
            offset = offset + jnp.sum(eq_f, axis=0, keepdims=True)
        return offset

    lax.fori_loop(0, n_kb, select_block, jnp.zeros((1, tq), F32))

    acc_ref[...] = jnp.zeros(acc_ref.shape, F32)

    def attn_block(kb, carry):
        ms, ls = carry
        start = pl.multiple_of(kb * kb_size, kb_size)
        bias = bias_ref[kb]
        new_m, new_l = [], []
        scores = []
        for h in range(B_HEADS):
            k2 = k_ref[pl.ds(start, kb_size), (h // 2) * LANES:(h // 2 + 1) * LANES]
            s = _dot_nt(k2, qm_ref[h]) + bias
            scores.append(s)
            new_m.append(jnp.maximum(ms[h], jnp.max(s, axis=0, keepdims=True)))
        for h in range(B_HEADS):
            alpha = jnp.exp2(ms[h] - new_m[h])
            p = jnp.exp2(scores[h] - new_m[h])
            new_l.append(alpha * ls[h] + jnp.sum(p, axis=0, keepdims=True))
            pv = _dot(vt_ref[kb, h * HEAD_DIM:(h + 1) * HEAD_DIM, :], p.astype(BF16))
            acc_ref[h] = alpha * acc_ref[h] + pv
        return tuple(new_m), tuple(new_l)

    m0 = tuple(jnp.full((1, tq), NEG, F32) for _ in range(B_HEADS))
    l0 = tuple(jnp.zeros((1, tq), F32) for _ in range(B_HEADS))
    _, ls = lax.fori_loop(0, n_kb, attn_block, (m0, l0))
    o_t = jnp.concatenate([acc_ref[h] / ls[h] for h in range(B_HEADS)], axis=0)
    o_ref[...] = o_t.T.astype(o_ref.dtype)


def _dsa_call(qk, vt, idx, iwt, batch, seq, topk):
    n = qk.shape[0]
    nq = seq // DSA_TQ
    width = B_HEADS * HEAD_DIM
    assert vt.shape[2] == DSA_TQ, "key blocks must match the projection's token tile"
    return pl.pallas_call(
        functools.partial(_dsa_kernel, topk=topk),
        grid=(batch, nq),
        in_specs=[
            pl.BlockSpec((DSA_TQ, 2 * LANES), lambda b, i: (b * nq + i, 0)),
            pl.BlockSpec((seq, LANES), lambda b, i: (b, 2)),
            pl.BlockSpec((IDX_HEADS, DSA_TQ), lambda b, i: (0, b * nq + i)),
            pl.BlockSpec((DSA_TQ, width), lambda b, i: (b * nq + i, 2)),
            pl.BlockSpec((seq, width), lambda b, i: (b, 3)),
            pl.BlockSpec((nq, width, DSA_TQ), lambda b, i: (b, 0, 0)),
        ],
        out_specs=pl.BlockSpec((DSA_TQ, width), lambda b, i: (b * nq + i, 0)),
        out_shape=jax.ShapeDtypeStruct((n, width), BF16),
        scratch_shapes=[
            pltpu.VMEM((IDX_HEADS, DSA_TQ, LANES), BF16),
            pltpu.VMEM((B_HEADS, DSA_TQ, LANES), BF16),
            pltpu.VMEM((nq, DSA_TQ, DSA_TQ), F32),
            pltpu.VMEM((nq, DSA_TQ, DSA_TQ), F32),
            pltpu.VMEM((B_HEADS, HEAD_DIM, DSA_TQ), F32),
        ],
        compiler_params=_params("parallel", "arbitrary"),
        name="dsa",
    )(idx, idx, iwt, qk, qk, vt)


def _band_attn_kernel(q_ref, k_ref, vt_ref, tbl_ref, o_ref):
    n_heads = tbl_ref.shape[0]
    tq = tbl_ref.shape[2]
    n_blk = tbl_ref.shape[1] // tq
    n_tiles = q_ref.shape[0] // tq
    lane = lax.broadcasted_iota(jnp.int32, (tq, LANES), 1)
    for c in range(n_tiles):
        blocks = [(d, c - (n_blk - 1) + d) for d in range(n_blk) if c - (n_blk - 1) + d >= 0]
        scores, ms = [], []
        for h in range(n_heads):
            sl = slice((h // 2) * LANES, (h // 2 + 1) * LANES)
            q2 = q_ref[c * tq:(c + 1) * tq, sl]
            qm = jnp.where((lane >> 6) == (h % 2), q2, jnp.zeros_like(q2))
            parts = [_dot_nt(k_ref[kb * tq:(kb + 1) * tq, sl], qm) + tbl_ref[h, d * tq:(d + 1) * tq, :]
                     for d, kb in blocks]
            scores.append(parts)
            ms.append(functools.reduce(jnp.maximum, [jnp.max(s, axis=0, keepdims=True) for s in parts]))
        outs = []
        for h in range(n_heads):
            ps = [jnp.exp2(s - ms[h]) for s in scores[h]]
            l = functools.reduce(jnp.add, [jnp.sum(p, axis=0, keepdims=True) for p in ps])
            rows = slice(h * HEAD_DIM, (h + 1) * HEAD_DIM)
            pv = functools.reduce(jnp.add, [_dot(vt_ref[kb, rows, :], p.astype(BF16))
                                            for (d, kb), p in zip(blocks, ps)])
            outs.append(pv / l)
        o_ref[c * tq:(c + 1) * tq, :] = jnp.concatenate(outs, axis=0).T.astype(o_ref.dtype)


def _band_attn_call(qk, vt, tbl, batch, seq):
    n = qk.shape[0]
    nq = seq // BAND_TQ
    n_groups = C_HEADS // BAND_HEADS
    width = BAND_HEADS * HEAD_DIM
    return pl.pallas_call(
        _band_attn_kernel,
        grid=(n_groups, batch),
        in_specs=[
            pl.BlockSpec((seq, width), lambda g, b: (b, g)),
            pl.BlockSpec((seq, width), lambda g, b: (b, n_groups + g)),
            pl.BlockSpec((nq, width, BAND_TQ), lambda g, b: (b, g, 0)),
            pl.BlockSpec((BAND_HEADS,) + tbl.shape[1:], lambda g, b: (g, 0, 0)),
        ],
        out_specs=pl.BlockSpec((seq, width), lambda g, b: (b, g)),
        out_shape=jax.ShapeDtypeStruct((n, D_MODEL), BF16),
        compiler_params=_params("parallel", "parallel"),
        name="band_attn",
    )(qk, qk, vt, tbl)


def _rope_tables(seq):
    def one(dim):
        inv = ROPE_THETA ** (-jnp.arange(0, dim, 2, dtype=F32) / dim)
        ang = jnp.arange(seq, dtype=F32)[:, None] * inv[None, :]
        cos, sin = jnp.cos(ang), jnp.sin(ang)
        reps = LANES // dim
        return (jnp.tile(jnp.concatenate([cos, cos], axis=-1), (1, reps)),
                jnp.tile(jnp.concatenate([-sin, sin], axis=-1), (1, reps)))
    c64, s64 = one(HEAD_DIM)
    c32, s32 = one(IDX_DIM)
    return jnp.stack([c64, s64, c32, s32])


def _even_w_in_layout(w):
    d = w.shape[0]
    n_qkv = 6 * PROJ_CC
    iq = w[:, n_qkv:n_qkv + IDX_HEADS * IDX_DIM]
    ik = w[:, n_qkv + IDX_HEADS * IDX_DIM:n_qkv + IDX_HEADS * IDX_DIM + IDX_DIM]
    iw = w[:, n_qkv + IDX_HEADS * IDX_DIM + IDX_DIM:]
    return jnp.concatenate(
        [w[:, :n_qkv], iq, jnp.tile(ik, (1, LANES // IDX_DIM)),
         iw, jnp.zeros((d, LANES - IDX_HEADS), w.dtype)], axis=1).astype(BF16)


def _band_table_kernel(g_ref, o_ref):
    window, tq = o_ref.shape[1], o_ref.shape[2]
    span = g_ref.shape[2]
    gb = jnp.broadcast_to(g_ref[0], (window, span))
    tbl = pltpu.roll(gb, span - window, 1, stride=1, stride_axis=0)[:, :tq]
    kc = lax.broadcasted_iota(jnp.int32, (window, tq), 0) >> 6
    qc = (window - tq + lax.broadcasted_iota(jnp.int32, (window, tq), 1)) >> 6
    in_band = (kc <= qc) & (kc >= qc - C_LEFT_CHUNKS)
    o_ref[0] = jnp.where(in_band, tbl * LOG2E, NEG)


def _band_bias_table(rel_bias):
    h = rel_bias.shape[0]
    left = C_LEFT_CHUNKS * CHUNK
    window = left + BAND_TQ
    span = BAND_TQ + window
    n_near = window - left + 1 - CHUNK
    n_far = span - n_near - REL_TABLE
    rel = rel_bias.astype(F32)
    g = jnp.concatenate([jnp.broadcast_to(rel[:, :1], (h, n_near)), rel,
                         jnp.broadcast_to(rel[:, REL_TABLE - 1:], (h, n_far))], axis=1)
    return pl.pallas_call(
        _band_table_kernel,
        grid=(h,),
        in_specs=[pl.BlockSpec((1, 1, span), lambda i: (i, 0, 0))],
        out_specs=pl.BlockSpec((1, window, BAND_TQ), lambda i: (i, 0, 0)),
        out_shape=jax.ShapeDtypeStruct((h, window, BAND_TQ), F32),
        compiler_params=_params("parallel"),
        name="band_table",
    )(g[:, None, :])


def kernel(x, norm_g, ffn_wg, ffn_wu, ffn_wd, even_w_in, even_w_out, even_lambda, even_subln,
           odd_w_in, odd_w_out, odd_rel_bias):
    batch, seq, d = x.shape
    depth = norm_g.shape[0]
    topk = min(TOPK_MAX, seq // 4)
    x2d = x.reshape(batch * seq, d)
    tabs = _rope_tables(seq)
    ffn_w = (ffn_wg.astype(BF16), ffn_wu.astype(BF16), ffn_wd.astype(BF16))
    pending = ((), ())
    g_mix = norm_g[0, 3]
    for l in range(depth):
        g = norm_g[l]
        x2d = _ffn_call(x2d, pending[0], pending[1], jnp.stack([g_mix, g[0], g[1]]), *ffn_w, (l, 0))
        if l % 2 == 0:
            e = l // 2
            lambda_init = 0.8 - 0.6 * math.exp(-0.3 * l)
            qk, avt, bvt, idx, iwt = _proj_even_call(x2d, g[2:3], _even_w_in_layout(even_w_in[e]), tabs, seq)
            o_a = _diff_attn_call(qk, avt, even_lambda[e], even_subln[e], batch, seq, lambda_init)
            o_b = _dsa_call(qk, bvt, idx, iwt, batch, seq, topk)
            w_out = even_w_out[e].astype(BF16)
            split = A_HEADS * A_VDIM
            pending = ((o_a, o_b), (w_out[:split], w_out[split:]))
        else:
            o = l // 2
            qk, vt = _proj_odd_call(x2d, g[2:3], odd_w_in[o].astype(BF16))
            o_c = _band_attn_call(qk, vt, _band_bias_table(odd_rel_bias[o]), batch, seq)
            pending = ((o_c,), (odd_w_out[o].astype(BF16),))
        g_mix = g[3]
        x2d = _ffn_call(x2d, pending[0], pending[1], jnp.stack([g_mix, g[4], g[5]]), *ffn_w, (l, 1))
        pending = ((), ())
    return x2d.reshape(batch, seq, d)
```

```python
import functools
import math

import jax
import jax.numpy as jnp
from jax import lax
from jax.experimental import pallas as pl
from jax.experimental.pallas import tpu as pltpu

D_MODEL = 1024
CHUNK = 64
HEAD_DIM = 64
ROPE_THETA = 10000.0
EPS = 1e-6
D_FF = 2816
A_HEADS = D_MODEL // (4 * HEAD_DIM)
A_VDIM = 2 * HEAD_DIM
B_HEADS = D_MODEL // (2 * HEAD_DIM)
IDX_HEADS = 8
IDX_DIM = 32
TOPK_MAX = 256
C_HEADS = D_MODEL // HEAD_DIM
C_LEFT_CHUNKS = 8
MAX_REL = 256
REL_TABLE = MAX_REL + CHUNK

LANES = 128
LOG2E = math.log2(math.e)
NEG = -1e30
VMEM_LIMIT = 56 * 1024 * 1024

FFN_TM = 512
FFN_FC = 256
PROJ_TM = 512
PROJ_CC = 512
DIFF_TQ = 256
DIFF_HEADS = 2
DSA_TQ = 512
BAND_TQ = 256
BAND_HEADS = 4

BF16 = jnp.bfloat16
F32 = jnp.float32


def _dot(a, b):
    return jnp.dot(a, b, preferred_element_type=F32)


def _dot_nt(a, b):
    return lax.dot_general(a, b, (((1,), (1,)), ((), ())), preferred_element_type=F32)


def _rms(xf, g):
    ms = jnp.mean(xf * xf, axis=-1, keepdims=True)
    return xf * lax.rsqrt(ms + EPS) * g


def _params(*sem):
    return pltpu.CompilerParams(dimension_semantics=sem, vmem_limit_bytes=VMEM_LIMIT)


def _ffn_kernel(*refs, n_mix, n_chunks):
    x_ref = refs[0]
    mix_refs = refs[1:1 + n_mix]
    wout_refs = refs[1 + n_mix:1 + 2 * n_mix]
    g_ref, wg_ref, wu_ref, wd_ref, out_ref = refs[1 + 2 * n_mix:]

    x = x_ref[...]
    if n_mix:
        m = _dot(mix_refs[0][...], wout_refs[0][...])
        for a_ref, w_ref in zip(mix_refs[1:], wout_refs[1:]):
            m = m + _dot(a_ref[...], w_ref[...])
        x = x + _rms(m, g_ref[0:1, :])
    h = _rms(x, g_ref[1:2, :]).astype(BF16)
    acc = jnp.zeros(x.shape, F32)
    for c in range(n_chunks):
        cols = slice(c * FFN_FC, (c + 1) * FFN_FC)
        a = _dot(h, wg_ref[:, cols])
        b = _dot(h, wu_ref[:, cols])
        t = (a / (1.0 + jnp.exp(-a))) * b
        acc = acc + _dot(t.astype(BF16), wd_ref[cols, :])
    out_ref[...] = x + 0.5 * _rms(acc, g_ref[2:3, :])


def _ffn_call(x2d, mixes, wouts, g3, wg, wu, wd, which):
    n, d = x2d.shape
    nc = D_FF // FFN_FC
    n_mix = len(mixes)
    row = lambda i: (i, 0)
    const2 = lambda i: (0, 0)
    in_specs = [pl.BlockSpec((FFN_TM, d), row)]
    in_specs += [pl.BlockSpec((FFN_TM, m.shape[1]), row) for m in mixes]
    in_specs += [pl.BlockSpec(w.shape, const2) for w in wouts]
    in_specs += [
        pl.BlockSpec(g3.shape, const2),
        pl.BlockSpec((None, None) + wg.shape[2:], lambda i: which + (0, 0)),
        pl.BlockSpec((None, None) + wu.shape[2:], lambda i: which + (0, 0)),
        pl.BlockSpec((None, None) + wd.shape[2:], lambda i: which + (0, 0)),
    ]
    return pl.pallas_call(
        functools.partial(_ffn_kernel, n_mix=n_mix, n_chunks=nc),
        grid=(n // FFN_TM,),
        in_specs=in_specs,
        out_specs=pl.BlockSpec((FFN_TM, d), row),
        out_shape=jax.ShapeDtypeStruct((n, d), F32),
        compiler_params=_params("parallel"),
        name="ffn_mix" if n_mix else "ffn",
    )(x2d, *mixes, *wouts, g3, wg, wu, wd)


def _rope(p, cos_t, sin_t, half):
    lane = lax.broadcasted_iota(jnp.int32, p.shape, 1)
    first = (lane & (2 * half - 1)) < half
    partner = jnp.where(first, pltpu.roll(p, LANES - half, 1), pltpu.roll(p, half, 1))
    return p * cos_t + partner * sin_t


def _proj_even_kernel(x_ref, g_ref, w_ref, tab_ref, qk_ref, avt_ref, bvt_ref, idx_ref, iwt_ref):
    h = _rms(x_ref[...], g_ref[0:1, :]).astype(BF16)
    c64, s64, c32, s32 = tab_ref[0], tab_ref[1], tab_ref[2], tab_ref[3]
    q_scale = HEAD_DIM ** -0.5 * LOG2E
    out_col = 0
    for c in (0, 1, 3, 4):
        p = _dot(h, w_ref[:, c * PROJ_CC:(c + 1) * PROJ_CC])
        for j in range(PROJ_CC // LANES):
            pj = _rope(p[:, j * LANES:(j + 1) * LANES], c64, s64, HEAD_DIM // 2)
            if c in (0, 3):
                pj = pj * q_scale
            qk_ref[:, out_col:out_col + LANES] = pj.astype(BF16)
            out_col += LANES
    avt_ref[...] = _dot(h, w_ref[:, 2 * PROJ_CC:3 * PROJ_CC]).T.astype(BF16)
    bvt_ref[0] = _dot(h, w_ref[:, 5 * PROJ_CC:6 * PROJ_CC]).T.astype(BF16)
    p = _dot(h, w_ref[:, 6 * PROJ_CC:7 * PROJ_CC])
    for j in range(3):
        pj = _rope(p[:, j * LANES:(j + 1) * LANES], c32, s32, IDX_DIM // 2)
        idx_ref[:, j * LANES:(j + 1) * LANES] = pj.astype(BF16)
    iw = p[:, 3 * LANES:4 * LANES] * (IDX_HEADS ** -0.5 * IDX_DIM ** -0.5)
    iwt_ref[...] = iw.T[0:IDX_HEADS, :]


def _proj_even_call(x2d, g, w, tabs, seq):
    n, d = x2d.shape
    steps_per_seq = seq // PROJ_TM
    return pl.pallas_call(
        _proj_even_kernel,
        grid=(n // PROJ_TM,),
        in_specs=[
            pl.BlockSpec((PROJ_TM, d), lambda i: (i, 0)),
            pl.BlockSpec(g.shape, lambda i: (0, 0)),
            pl.BlockSpec(w.shape, lambda i: (0, 0)),
            pl.BlockSpec((4, PROJ_TM, LANES), lambda i: (0, i % steps_per_seq, 0)),
        ],
        out_specs=[
            pl.BlockSpec((PROJ_TM, 4 * PROJ_CC), lambda i: (i, 0)),
            pl.BlockSpec((PROJ_CC, PROJ_TM), lambda i: (0, i)),
            pl.BlockSpec((1, PROJ_CC, PROJ_TM), lambda i: (i, 0, 0)),
            pl.BlockSpec((PROJ_TM, 3 * LANES), lambda i: (i, 0)),
            pl.BlockSpec((IDX_HEADS, PROJ_TM), lambda i: (0, i)),
        ],
        out_shape=[
            jax.ShapeDtypeStruct((n, 4 * PROJ_CC), BF16),
            jax.ShapeDtypeStruct((PROJ_CC, n), BF16),
            jax.ShapeDtypeStruct((n // PROJ_TM, PROJ_CC, PROJ_TM), BF16),
            jax.ShapeDtypeStruct((n, 3 * LANES), BF16),
            jax.ShapeDtypeStruct((IDX_HEADS, n), F32),
        ],
        compiler_params=_params("parallel"),
        name="proj_even",
    )(x2d, g, w, tabs)


def _proj_odd_kernel(x_ref, g_ref, w_ref, qk_ref, vt_ref):
    h = _rms(x_ref[...], g_ref[0:1, :]).astype(BF16)
    q_scale = HEAD_DIM ** -0.5 * LOG2E
    for c in range(2 * D_MODEL // PROJ_CC):
        p = _dot(h, w_ref[:, c * PROJ_CC:(c + 1) * PROJ_CC])
        if c * PROJ_CC < D_MODEL:
            p = p * q_scale
        qk_ref[:, c * PROJ_CC:(c + 1) * PROJ_CC] = p.astype(BF16)
    for c in range(D_MODEL // PROJ_CC):
        col = 2 * D_MODEL + c * PROJ_CC
        vt = _dot(h, w_ref[:, col:col + PROJ_CC]).T.astype(BF16)
        for j in range(vt_ref.shape[0]):
            vt_ref[j, c * PROJ_CC:(c + 1) * PROJ_CC, :] = vt[:, j * BAND_TQ:(j + 1) * BAND_TQ]


def _proj_odd_call(x2d, g, w):
    n, d = x2d.shape
    return pl.pallas_call(
        _proj_odd_kernel,
        grid=(n // PROJ_TM,),
        in_specs=[
            pl.BlockSpec((PROJ_TM, d), lambda i: (i, 0)),
            pl.BlockSpec(g.shape, lambda i: (0, 0)),
            pl.BlockSpec(w.shape, lambda i: (0, 0)),
        ],
        out_specs=[
            pl.BlockSpec((PROJ_TM, 2 * d), lambda i: (i, 0)),
            pl.BlockSpec((PROJ_TM // BAND_TQ, d, BAND_TQ), lambda i: (i, 0, 0)),
        ],
        out_shape=[
            jax.ShapeDtypeStruct((n, 2 * d), BF16),
            jax.ShapeDtypeStruct((n // BAND_TQ, d, BAND_TQ), BF16),
        ],
        compiler_params=_params("parallel"),
        name="proj_odd",
    )(x2d, g, w)


def _diff_attn_tile(c, tq, q_ref, k_ref, vt_ref, lam, sub_ref, o_ref, lambda_init):
    n_heads = q_ref.shape[1] // LANES
    left = c * tq
    lane = lax.broadcasted_iota(jnp.int32, (tq, LANES), 1)
    key_chunk = lax.broadcasted_iota(jnp.int32, (tq, tq), 0) >> 6
    qry_chunk = lax.broadcasted_iota(jnp.int32, (tq, tq), 1) >> 6
    diag_mask = key_chunk <= qry_chunk
    raw = []
    for h in range(n_heads):
        sl = slice(h * LANES, (h + 1) * LANES)
        q = q_ref[left:left + tq, sl]
        for keep in (lane < HEAD_DIM, lane >= HEAD_DIM):
            qc = jnp.where(keep, q, jnp.zeros_like(q))
            parts = [jnp.where(diag_mask, _dot_nt(k_ref[left:left + tq, sl], qc), NEG)]
            if c:
                parts.append(_dot_nt(k_ref[0:left, sl], qc))
            raw.append(parts)
    comps = []
    for parts in raw:
        m = functools.reduce(jnp.maximum, [jnp.max(s, axis=0, keepdims=True) for s in parts])
        parts = [jnp.exp2(s - m) for s in parts]
        l = functools.reduce(jnp.add, [jnp.sum(p, axis=0, keepdims=True) for p in parts])
        comps.append((parts, l))
    for h in range(n_heads):
        rows = slice(h * A_VDIM, (h + 1) * A_VDIM)
        outs = []
        for parts, l in (comps[2 * h], comps[2 * h + 1]):
            acc = _dot(vt_ref[rows, left:left + tq], parts[0].astype(BF16))
            if c:
                acc = acc + _dot(vt_ref[rows, 0:left], parts[1].astype(BF16))
            outs.append(acc / l)
        o_t = outs[0] - lam * outs[1]
        ms = jnp.mean(o_t * o_t, axis=0, keepdims=True)
        o_t = o_t * lax.rsqrt(ms + EPS) * sub_ref[...] * (1.0 - lambda_init)
        o_ref[left:left + tq, h * LANES:(h + 1) * LANES] = o_t.T.astype(o_ref.dtype)


def _diff_attn_kernel(q_ref, k_ref, vt_ref, lam_ref, sub_ref, o_ref, *, lambda_init, tq):
    lp = lam_ref[...]
    lam = (jnp.exp(jnp.sum(lp[0:1] * lp[1:2], axis=-1, keepdims=True))
           - jnp.exp(jnp.sum(lp[2:3] * lp[3:4], axis=-1, keepdims=True)) + lambda_init)
    for c in range(q_ref.shape[0] // tq):
        _diff_attn_tile(c, tq, q_ref, k_ref, vt_ref, lam, sub_ref, o_ref, lambda_init)


def _diff_attn_call(qk, avt, lam_p, subln, batch, seq, lambda_init):
    n = qk.shape[0]
    n_groups = A_HEADS // DIFF_HEADS
    width = DIFF_HEADS * A_VDIM
    sub_t = jnp.broadcast_to(subln.astype(F32)[:, None], (A_VDIM, DIFF_TQ))
    return pl.pallas_call(
        functools.partial(_diff_attn_kernel, lambda_init=lambda_init, tq=DIFF_TQ),
        grid=(batch, n_groups),
        in_specs=[
            pl.BlockSpec((seq, width), lambda b, g: (b, g)),
            pl.BlockSpec((seq, width), lambda b, g: (b, n_groups + g)),
            pl.BlockSpec((width, seq), lambda b, g: (g, b)),
            pl.BlockSpec(lam_p.shape, lambda b, g: (0, 0)),
            pl.BlockSpec(sub_t.shape, lambda b, g: (0, 0)),
        ],
        out_specs=pl.BlockSpec((seq, width), lambda b, g: (b, g)),
        out_shape=jax.ShapeDtypeStruct((n, A_HEADS * A_VDIM), BF16),
        compiler_params=_params("parallel", "parallel"),
        name="diff_attn",
    )(qk, qk, avt, lam_p, sub_t)


def _dsa_kernel(iq_ref, ik_ref, iwt_ref, q_ref, k_ref, vt_ref, o_ref,
                qh_ref, qm_ref, score_ref, coarse_ref, acc_ref, *, topk):
    i = pl.program_id(1)
    tq = q_ref.shape[0]
    kb_size = score_ref.shape[1]
    n_kb = i + 1
    kf = float(topk)

    lane = lax.broadcasted_iota(jnp.int32, (tq, LANES), 1)
    idx_per_vreg = LANES // IDX_DIM
    for h in range(IDX_HEADS):
        x = iq_ref[:, (h // idx_per_vreg) * LANES:(h // idx_per_vreg + 1) * LANES]
        qh_ref[h] = jnp.where((lane >> 5) == (h % idx_per_vreg), x, jnp.zeros_like(x))
    for h in range(B_HEADS):
        x = q_ref[:, (h // 2) * LANES:(h // 2 + 1) * LANES]
        qm_ref[h] = jnp.where((lane >> 6) == (h % 2), x, jnp.zeros_like(x))

    q_chunk = (i * tq + lax.broadcasted_iota(jnp.int32, (kb_size, tq), 1)) >> 6

    def visible(kb):
        k_chunk = (kb * kb_size + lax.broadcasted_iota(jnp.int32, (kb_size, tq), 0)) >> 6
        return k_chunk <= q_chunk

    def score_block(kb, carry):
        start = pl.multiple_of(kb * kb_size, kb_size)
        ikb = ik_ref[pl.ds(start, kb_size), :]
        dots = [_dot_nt(ikb, qh_ref[h]) for h in range(IDX_HEADS)]
        score = jnp.maximum(dots[0], 0.0) * iwt_ref[0:1, :]
        for h in range(1, IDX_HEADS):
            score = score + jnp.maximum(dots[h], 0.0) * iwt_ref[h:h + 1, :]
        score = jnp.where(visible(kb), score + 0.0, -jnp.inf)
        score_ref[kb] = score
        coarse_ref[kb] = score.astype(BF16)
        return carry

    lax.fori_loop(0, n_kb, score_block, 0)

    def count(pred):
        def block(kb, acc):
            for r in range(kb_size // 8):
                x = score_ref[kb, r * 8:(r + 1) * 8, :]
                acc = acc + jnp.where(pred(x), 1.0, 0.0)
            return acc
        acc = lax.fori_loop(0, n_kb, block, jnp.zeros((8, tq), F32))
        return jnp.sum(acc, axis=0, keepdims=True)

    pack = 16
    one_b = jnp.ones((pack, tq), BF16)
    zero_b = jnp.zeros((pack, tq), BF16)

    def count_coarse(pred):
        def block(kb, acc):
            for r in range(kb_size // pack):
                x = coarse_ref[kb, r * pack:(r + 1) * pack, :]
                acc = acc + jnp.where(pred(x), one_b, zero_b)
            return acc
        acc = lax.fori_loop(0, n_kb, block, zero_b)
        return jnp.sum(acc.astype(F32), axis=0, keepdims=True)

    def key_to_float(key):
        return pltpu.bitcast(key ^ ((key >> 31) & 0x7FFFFFFF), F32)

    def key16_to_key(k16):
        return lax.shift_left(k16, 16) | ((k16 >> 31) & 0xFFFF)

    neg_inf_key16 = -32641
    neg_inf_key = -2139095041
    top_key16 = 32767

    def coarse_step(it, t):
        cand = t + lax.shift_left(jnp.int32(1), 15 - it)
        c_f = key_to_float(key16_to_key(jnp.minimum(cand, top_key16)))
        c_b = jnp.broadcast_to(c_f, (pack, tq)).astype(BF16)
        cnt = count_coarse(lambda x: x >= c_b)
        return jnp.where((cand <= top_key16) & (cnt >= kf), cand, t)

    t16 = lax.fori_loop(0, 16, coarse_step, jnp.full((1, tq), neg_inf_key16, jnp.int32))

    def fine_step(it, t):
        cand = t + lax.shift_left(jnp.int32(1), 16 - it)
        c_f = key_to_float(cand)
        cnt = count(lambda x: x >= c_f)
        return jnp.where((cand > t) & (cnt >= kf), cand, t)

    thr_key = lax.fori_loop(0, 17, fine_step, jnp.maximum(key16_to_key(t16 - 1), neg_inf_key))
    thr = key_to_float(thr_key)
    need = kf - count(lambda x: x > thr)

    r_i = lax.broadcasted_iota(jnp.int32, (LANES, LANES), 0)
    c_i = lax.broadcasted_iota(jnp.int32, (LANES, LANES), 1)
    strict_lower = jnp.where(c_i < r_i, 1.0, 0.0).astype(BF16)

    def select_bias(kb, offset):
        vis = visible(kb)
        rows = []
        for j in range(kb_size // LANES):
            sl = slice(j * LANES, (j + 1) * LANES)
            sj = score_ref[kb, sl, :]
            eq = sj == thr
            eq_f = jnp.where(eq, 1.0, 0.0)
            before = _dot(strict_lower, eq_f.astype(BF16)) + offset
            sel = ((sj > thr) | (eq & (before < need))) & vis[sl]
            rows.append(jnp.where(sel, 0.0, NEG))
            offset = offset + jnp.sum(eq_f, axis=0, keepdims=True)
        return jnp.concatenate(rows, axis=0), offset

    acc_ref[...] = jnp.zeros(acc_ref.shape, F32)

    def attn_block(kb, carry):
        ms, ls, offset = carry
        start = pl.multiple_of(kb * kb_size, kb_size)
        bias, offset = select_bias(kb, offset)
        new_m, new_l = [], []
        scores = []
        for h in range(B_HEADS):
            k2 = k_ref[pl.ds(start, kb_size), (h // 2) * LANES:(h // 2 + 1) * LANES]
            s = _dot_nt(k2, qm_ref[h]) + bias
            scores.append(s)
            new_m.append(jnp.maximum(ms[h], jnp.max(s, axis=0, keepdims=True)))
        for h in range(B_HEADS):
            alpha = jnp.exp2(ms[h] - new_m[h])
            p = jnp.exp2(scores[h] - new_m[h])
            new_l.append(alpha * ls[h] + jnp.sum(p, axis=0, keepdims=True))
            pv = _dot(vt_ref[kb, h * HEAD_DIM:(h + 1) * HEAD_DIM, :], p.astype(BF16))
            acc_ref[h] = alpha * acc_ref[h] + pv
        return tuple(new_m), tuple(new_l), offset

    m0 = tuple(jnp.full((1, tq), NEG, F32) for _ in range(B_HEADS))
    l0 = tuple(jnp.zeros((1, tq), F32) for _ in range(B_HEADS))
    _, ls, _ = lax.fori_loop(0, n_kb, attn_block, (m0, l0, jnp.zeros((1, tq), F32)))
    o_t = jnp.concatenate([acc_ref[h] / ls[h] for h in range(B_HEADS)], axis=0)
    o_ref[...] = o_t.T.astype(o_ref.dtype)


def _dsa_call(qk, vt, idx, iwt, batch, seq, topk):
    n = qk.shape[0]
    nq = seq // DSA_TQ
    width = B_HEADS * HEAD_DIM
    assert vt.shape[2] == DSA_TQ, "key blocks must match the projection's token tile"
    return pl.pallas_call(
        functools.partial(_dsa_kernel, topk=topk),
        grid=(batch, nq),
        in_specs=[
            pl.BlockSpec((DSA_TQ, 2 * LANES), lambda b, i: (b * nq + i, 0)),
            pl.BlockSpec((seq, LANES), lambda b, i: (b, 2)),
            pl.BlockSpec((IDX_HEADS, DSA_TQ), lambda b, i: (0, b * nq + i)),
            pl.BlockSpec((DSA_TQ, width), lambda b, i: (b * nq + i, 2)),
            pl.BlockSpec((seq, width), lambda b, i: (b, 3)),
            pl.BlockSpec((nq, width, DSA_TQ), lambda b, i: (b, 0, 0)),
        ],
        out_specs=pl.BlockSpec((DSA_TQ, width), lambda b, i: (b * nq + i, 0)),
        out_shape=jax.ShapeDtypeStruct((n, width), BF16),
        scratch_shapes=[
            pltpu.VMEM((IDX_HEADS, DSA_TQ, LANES), BF16),
            pltpu.VMEM((B_HEADS, DSA_TQ, LANES), BF16),
            pltpu.VMEM((nq, DSA_TQ, DSA_TQ), F32),
            pltpu.VMEM((nq, DSA_TQ, DSA_TQ), BF16),
            pltpu.VMEM((B_HEADS, HEAD_DIM, DSA_TQ), F32),
        ],
        compiler_params=_params("parallel", "arbitrary"),
        name="dsa",
    )(idx, idx, iwt, qk, qk, vt)


def _band_attn_kernel(q_ref, k_ref, vt_ref, tbl_ref, o_ref):
    n_heads = tbl_ref.shape[0]
    tq = tbl_ref.shape[2]
    n_blk = tbl_ref.shape[1] // tq
    n_tiles = q_ref.shape[0] // tq
    lane = lax.broadcasted_iota(jnp.int32, (tq, LANES), 1)
    for c in range(n_tiles):
        blocks = [(d, c - (n_blk - 1) + d) for d in range(n_blk) if c - (n_blk - 1) + d >= 0]
        scores, ms = [], []
        for h in range(n_heads):
            sl = slice((h // 2) * LANES, (h // 2 + 1) * LANES)
            q2 = q_ref[c * tq:(c + 1) * tq, sl]
            qm = jnp.where((lane >> 6) == (h % 2), q2, jnp.zeros_like(q2))
            parts = [_dot_nt(k_ref[kb * tq:(kb + 1) * tq, sl], qm) + tbl_ref[h, d * tq:(d + 1) * tq, :]
                     for d, kb in blocks]
            scores.append(parts)
            ms.append(functools.reduce(jnp.maximum, [jnp.max(s, axis=0, keepdims=True) for s in parts]))
        outs = []
        for h in range(n_heads):
            ps = [jnp.exp2(s - ms[h]) for s in scores[h]]
            l = functools.reduce(jnp.add, [jnp.sum(p, axis=0, keepdims=True) for p in ps])
            rows = slice(h * HEAD_DIM, (h + 1) * HEAD_DIM)
            pv = functools.reduce(jnp.add, [_dot(vt_ref[kb, rows, :], p.astype(BF16))
                                            for (d, kb), p in zip(blocks, ps)])
            outs.append(pv / l)
        o_ref[c * tq:(c + 1) * tq, :] = jnp.concatenate(outs, axis=0).T.astype(o_ref.dtype)


def _band_attn_call(qk, vt, tbl, batch, seq):
    n = qk.shape[0]
    nq = seq // BAND_TQ
    n_groups = C_HEADS // BAND_HEADS
    width = BAND_HEADS * HEAD_DIM
    return pl.pallas_call(
        _band_attn_kernel,
        grid=(n_groups, batch),
        in_specs=[
            pl.BlockSpec((seq, width), lambda g, b: (b, g)),
            pl.BlockSpec((seq, width), lambda g, b: (b, n_groups + g)),
            pl.BlockSpec((nq, width, BAND_TQ), lambda g, b: (b, g, 0)),
            pl.BlockSpec((BAND_HEADS,) + tbl.shape[1:], lambda g, b: (g, 0, 0)),
        ],
        out_specs=pl.BlockSpec((seq, width), lambda g, b: (b, g)),
        out_shape=jax.ShapeDtypeStruct((n, D_MODEL), BF16),
        compiler_params=_params("parallel", "parallel"),
        name="band_attn",
    )(qk, qk, vt, tbl)


def _rope_tables(seq):
    def one(dim):
        inv = ROPE_THETA ** (-jnp.arange(0, dim, 2, dtype=F32) / dim)
        ang = jnp.arange(seq, dtype=F32)[:, None] * inv[None, :]
        cos, sin = jnp.cos(ang), jnp.sin(ang)
        reps = LANES // dim
        return (jnp.tile(jnp.concatenate([cos, cos], axis=-1), (1, reps)),
                jnp.tile(jnp.concatenate([-sin, sin], axis=-1), (1, reps)))
    c64, s64 = one(HEAD_DIM)
    c32, s32 = one(IDX_DIM)
    return jnp.stack([c64, s64, c32, s32])


def _even_w_in_layout(w):
    d = w.shape[0]
    n_qkv = 6 * PROJ_CC
    iq = w[:, n_qkv:n_qkv + IDX_HEADS * IDX_DIM]
    ik = w[:, n_qkv + IDX_HEADS * IDX_DIM:n_qkv + IDX_HEADS * IDX_DIM + IDX_DIM]
    iw = w[:, n_qkv + IDX_HEADS * IDX_DIM + IDX_DIM:]
    return jnp.concatenate(
        [w[:, :n_qkv], iq, jnp.tile(ik, (1, LANES // IDX_DIM)),
         iw, jnp.zeros((d, LANES - IDX_HEADS), w.dtype)], axis=1).astype(BF16)


def _band_table_kernel(g_ref, o_ref):
    window, tq = o_ref.shape[1], o_ref.shape[2]
    span = g_ref.shape[2]
    gb = jnp.broadcast_to(g_ref[0], (window, span))
    tbl = pltpu.roll(gb, span - window, 1, stride=1, stride_axis=0)[:, :tq]
    kc = lax.broadcasted_iota(jnp.int32, (window, tq), 0) >> 6
    qc = (window - tq + lax.broadcasted_iota(jnp.int32, (window, tq), 1)) >> 6
    in_band = (kc <= qc) & (kc >= qc - C_LEFT_CHUNKS)
    o_ref[0] = jnp.where(in_band, tbl * LOG2E, NEG)


def _band_bias_table(rel_bias):
    h = rel_bias.shape[0]
    left = C_LEFT_CHUNKS * CHUNK
    window = left + BAND_TQ
    span = BAND_TQ + window
    n_near = window - left + 1 - CHUNK
    n_far = span - n_near - REL_TABLE
    rel = rel_bias.astype(F32)
    g = jnp.concatenate([jnp.broadcast_to(rel[:, :1], (h, n_near)), rel,
                         jnp.broadcast_to(rel[:, REL_TABLE - 1:], (h, n_far))], axis=1)
    return pl.pallas_call(
        _band_table_kernel,
        grid=(h,),
        in_specs=[pl.BlockSpec((1, 1, span), lambda i: (i, 0, 0))],
        out_specs=pl.BlockSpec((1, window, BAND_TQ), lambda i: (i, 0, 0)),
        out_shape=jax.ShapeDtypeStruct((h, window, BAND_TQ), F32),
        compiler_params=_params("parallel"),
        name="band_table",
    )(g[:, None, :])


def kernel(x, norm_g, ffn_wg, ffn_wu, ffn_wd, even_w_in, even_w_out, even_lambda, even_subln,
           odd_w_in, odd_w_out, odd_rel_bias):
    batch, seq, d = x.shape
    depth = norm_g.shape[0]
    topk = min(TOPK_MAX, seq // 4)
    x2d = x.reshape(batch * seq, d)
    tabs = _rope_tables(seq)
    ffn_w = (ffn_wg.astype(BF16), ffn_wu.astype(BF16), ffn_wd.astype(BF16))
    pending = ((), ())
    g_mix = norm_g[0, 3]
    for l in range(depth):
        g = norm_g[l]
        x2d = _ffn_call(x2d, pending[0], pending[1], jnp.stack([g_mix, g[0], g[1]]), *ffn_w, (l, 0))
        if l % 2 == 0:
            e = l // 2
            lambda_init = 0.8 - 0.6 * math.exp(-0.3 * l)
            qk, avt, bvt, idx, iwt = _proj_even_call(x2d, g[2:3], _even_w_in_layout(even_w_in[e]), tabs, seq)
            o_a = _diff_attn_call(qk, avt, even_lambda[e], even_subln[e], batch, seq, lambda_init)
            o_b = _dsa_call(qk, bvt, idx, iwt, batch, seq, topk)
            w_out = even_w_out[e].astype(BF16)
            split = A_HEADS * A_VDIM
            pending = ((o_a, o_b), (w_out[:split], w_out[split:]))
        else:
            o = l // 2
            qk, vt = _proj_odd_call(x2d, g[2:3], odd_w_in[o].astype(BF16))
            o_c = _band_attn_call(qk, vt, _band_bias_table(odd_rel_bias[o]), batch, seq)
            pending = ((o_c,), (odd_w_out[o].astype(BF16),))
        g_mix = g[3]
        x2d = _ffn_call(x2d, pending[0], pending[1], jnp.stack([g_mix, g[4], g[5]]), *ffn_w, (l, 1))
        pending = ((), ())
    return x2d.reshape(batch, seq, d)
```

```python
import functools
import math

import jax
import jax.numpy as jnp
from jax import lax
from jax.experimental import pallas as pl
from jax.experimental.pallas import tpu as pltpu

D_MODEL = 1024
CHUNK = 64
HEAD_DIM = 64
ROPE_THETA = 10000.0
EPS = 1e-6
D_FF = 2816
A_HEADS = D_MODEL // (4 * HEAD_DIM)
A_VDIM = 2 * HEAD_DIM
B_HEADS = D_MODEL // (2 * HEAD_DIM)
IDX_HEADS = 8
IDX_DIM = 32
TOPK_MAX = 256
C_HEADS = D_MODEL // HEAD_DIM
C_LEFT_CHUNKS = 8
MAX_REL = 256
REL_TABLE = MAX_REL + CHUNK

LANES = 128
LOG2E = math.log2(math.e)
NEG = -1e30
VMEM_LIMIT = 56 * 1024 * 1024

FFN_TM = 1024
FFN_FC = 256
PROJ_TM = 512
PROJ_CC = 512
DIFF_TQ = 256
DIFF_HEADS = 2
DSA_TQ = 512
BAND_TQ = 256
BAND_HEADS = 4

BF16 = jnp.bfloat16
F32 = jnp.float32


def _dot(a, b):
    return jnp.dot(a, b, preferred_element_type=F32)


def _dot_nt(a, b):
    return lax.dot_general(a, b, (((1,), (1,)), ((), ())), preferred_element_type=F32)


def _rms(xf, g):
    ms = jnp.mean(xf * xf, axis=-1, keepdims=True)
    return xf * lax.rsqrt(ms + EPS) * g


def _params(*sem):
    return pltpu.CompilerParams(dimension_semantics=sem, vmem_limit_bytes=VMEM_LIMIT)


def _ffn_kernel(*refs, n_mix, n_chunks):
    x_ref = refs[0]
    mix_refs = refs[1:1 + n_mix]
    wout_refs = refs[1 + n_mix:1 + 2 * n_mix]
    g_ref, wg_ref, wu_ref, wd_ref, out_ref = refs[1 + 2 * n_mix:]

    x = x_ref[...]
    if n_mix:
        m = _dot(mix_refs[0][...], wout_refs[0][...])
        for a_ref, w_ref in zip(mix_refs[1:], wout_refs[1:]):
            m = m + _dot(a_ref[...], w_ref[...])
        x = x + _rms(m, g_ref[0:1, :])
    h = _rms(x, g_ref[1:2, :]).astype(BF16)
    acc = jnp.zeros(x.shape, F32)
    for c in range(n_chunks):
        cols = slice(c * FFN_FC, (c + 1) * FFN_FC)
        a = _dot(h, wg_ref[:, cols])
        b = _dot(h, wu_ref[:, cols])
        t = (a / (1.0 + jnp.exp(-a))) * b
        acc = acc + _dot(t.astype(BF16), wd_ref[cols, :])
    out_ref[...] = x + 0.5 * _rms(acc, g_ref[2:3, :])


def _ffn_call(x2d, mixes, w_out, g3, wg, wu, wd, which):
    n, d = x2d.shape
    nc = D_FF // FFN_FC
    n_mix = len(mixes)
    row = lambda i: (i, 0)
    const2 = lambda i: (0, 0)
    in_specs = [pl.BlockSpec((FFN_TM, d), row)]
    in_specs += [pl.BlockSpec((FFN_TM, m.shape[1]), row) for m in mixes]
    once = pl.Buffered(1)
    row0 = 0
    for m in mixes:
        in_specs.append(pl.BlockSpec((m.shape[1], d), lambda i, r=row0 // m.shape[1]: (r, 0),
                                     pipeline_mode=once))
        row0 += m.shape[1]
    in_specs += [
        pl.BlockSpec(g3.shape, const2),
        pl.BlockSpec((None, None) + wg.shape[2:], lambda i: which + (0, 0), pipeline_mode=once),
        pl.BlockSpec((None, None) + wu.shape[2:], lambda i: which + (0, 0), pipeline_mode=once),
        pl.BlockSpec((None, None) + wd.shape[2:], lambda i: which + (0, 0), pipeline_mode=once),
    ]
    return pl.pallas_call(
        functools.partial(_ffn_kernel, n_mix=n_mix, n_chunks=nc),
        grid=(n // FFN_TM,),
        in_specs=in_specs,
        out_specs=pl.BlockSpec((FFN_TM, d), row),
        out_shape=jax.ShapeDtypeStruct((n, d), F32),
        compiler_params=_params("parallel"),
        name="ffn_mix" if n_mix else "ffn",
    )(x2d, *mixes, *([w_out] * n_mix), g3, wg, wu, wd)


def _rope(p, cos_t, sin_t, half):
    lane = lax.broadcasted_iota(jnp.int32, p.shape, 1)
    first = (lane & (2 * half - 1)) < half
    partner = jnp.where(first, pltpu.roll(p, LANES - half, 1), pltpu.roll(p, half, 1))
    return p * cos_t + partner * sin_t


def _proj_even_kernel(x_ref, g_ref, w_ref, tab_ref, qk_ref, avt_ref, bvt_ref, idx_ref, iwt_ref):
    h = _rms(x_ref[...], g_ref[0:1, :]).astype(BF16)
    c64, s64, c32, s32 = tab_ref[0], tab_ref[1], tab_ref[2], tab_ref[3]
    q_scale = HEAD_DIM ** -0.5 * LOG2E
    out_col = 0
    for c in (0, 1, 3, 4):
        p = _dot(h, w_ref[:, c * PROJ_CC:(c + 1) * PROJ_CC])
        for j in range(PROJ_CC // LANES):
            pj = _rope(p[:, j * LANES:(j + 1) * LANES], c64, s64, HEAD_DIM // 2)
            if c in (0, 3):
                pj = pj * q_scale
            qk_ref[:, out_col:out_col + LANES] = pj.astype(BF16)
            out_col += LANES
    avt_ref[...] = _dot(h, w_ref[:, 2 * PROJ_CC:3 * PROJ_CC]).T.astype(BF16)
    bvt_ref[0] = _dot(h, w_ref[:, 5 * PROJ_CC:6 * PROJ_CC]).T.astype(BF16)
    p = _dot(h, w_ref[:, 6 * PROJ_CC:7 * PROJ_CC])
    for j in range(3):
        pj = _rope(p[:, j * LANES:(j + 1) * LANES], c32, s32, IDX_DIM // 2)
        idx_ref[:, j * LANES:(j + 1) * LANES] = pj.astype(BF16)
    iw = p[:, 3 * LANES:4 * LANES] * (IDX_HEADS ** -0.5 * IDX_DIM ** -0.5)
    iwt_ref[...] = iw.T[0:IDX_HEADS, :]


def _proj_even_call(x2d, g, w, tabs, seq):
    n, d = x2d.shape
    steps_per_seq = seq // PROJ_TM
    return pl.pallas_call(
        _proj_even_kernel,
        grid=(n // PROJ_TM,),
        in_specs=[
            pl.BlockSpec((PROJ_TM, d), lambda i: (i, 0)),
            pl.BlockSpec(g.shape, lambda i: (0, 0)),
            pl.BlockSpec(w.shape, lambda i: (0, 0)),
            pl.BlockSpec((4, PROJ_TM, LANES), lambda i: (0, i % steps_per_seq, 0)),
        ],
        out_specs=[
            pl.BlockSpec((PROJ_TM, 4 * PROJ_CC), lambda i: (i, 0)),
            pl.BlockSpec((PROJ_CC, PROJ_TM), lambda i: (0, i)),
            pl.BlockSpec((1, PROJ_CC, PROJ_TM), lambda i: (i, 0, 0)),
            pl.BlockSpec((PROJ_TM, 3 * LANES), lambda i: (i, 0)),
            pl.BlockSpec((IDX_HEADS, PROJ_TM), lambda i: (0, i)),
        ],
        out_shape=[
            jax.ShapeDtypeStruct((n, 4 * PROJ_CC), BF16),
            jax.ShapeDtypeStruct((PROJ_CC, n), BF16),
            jax.ShapeDtypeStruct((n // PROJ_TM, PROJ_CC, PROJ_TM), BF16),
            jax.ShapeDtypeStruct((n, 3 * LANES), BF16),
            jax.ShapeDtypeStruct((IDX_HEADS, n), F32),
        ],
        compiler_params=_params("parallel"),
        name="proj_even",
    )(x2d, g, w, tabs)


def _proj_odd_kernel(x_ref, g_ref, w_ref, qk_ref, vt_ref):
    h = _rms(x_ref[...], g_ref[0:1, :]).astype(BF16)
    q_scale = HEAD_DIM ** -0.5 * LOG2E
    for c in range(2 * D_MODEL // PROJ_CC):
        p = _dot(h, w_ref[:, c * PROJ_CC:(c + 1) * PROJ_CC])
        if c * PROJ_CC < D_MODEL:
            p = p * q_scale
        qk_ref[:, c * PROJ_CC:(c + 1) * PROJ_CC] = p.astype(BF16)
    for c in range(D_MODEL // PROJ_CC):
        col = 2 * D_MODEL + c * PROJ_CC
        vt = _dot(h, w_ref[:, col:col + PROJ_CC]).T.astype(BF16)
        for j in range(vt_ref.shape[0]):
            vt_ref[j, c * PROJ_CC:(c + 1) * PROJ_CC, :] = vt[:, j * BAND_TQ:(j + 1) * BAND_TQ]


def _proj_odd_call(x2d, g, w):
    n, d = x2d.shape
    return pl.pallas_call(
        _proj_odd_kernel,
        grid=(n // PROJ_TM,),
        in_specs=[
            pl.BlockSpec((PROJ_TM, d), lambda i: (i, 0)),
            pl.BlockSpec(g.shape, lambda i: (0, 0)),
            pl.BlockSpec(w.shape, lambda i: (0, 0)),
        ],
        out_specs=[
            pl.BlockSpec((PROJ_TM, 2 * d), lambda i: (i, 0)),
            pl.BlockSpec((PROJ_TM // BAND_TQ, d, BAND_TQ), lambda i: (i, 0, 0)),
        ],
        out_shape=[
            jax.ShapeDtypeStruct((n, 2 * d), BF16),
            jax.ShapeDtypeStruct((n // BAND_TQ, d, BAND_TQ), BF16),
        ],
        compiler_params=_params("parallel"),
        name="proj_odd",
    )(x2d, g, w)


def _diff_attn_tile(c, tq, q_ref, k_ref, vt_ref, lam, sub_ref, o_ref, lambda_init):
    n_heads = q_ref.shape[1] // LANES
    left = c * tq
    lane = lax.broadcasted_iota(jnp.int32, (tq, LANES), 1)
    key_chunk = lax.broadcasted_iota(jnp.int32, (tq, tq), 0) >> 6
    qry_chunk = lax.broadcasted_iota(jnp.int32, (tq, tq), 1) >> 6
    diag_mask = key_chunk <= qry_chunk
    raw = []
    for h in range(n_heads):
        sl = slice(h * LANES, (h + 1) * LANES)
        q = q_ref[left:left + tq, sl]
        for keep in (lane < HEAD_DIM, lane >= HEAD_DIM):
            qc = jnp.where(keep, q, jnp.zeros_like(q))
            parts = [jnp.where(diag_mask, _dot_nt(k_ref[left:left + tq, sl], qc), NEG)]
            if c:
                parts.append(_dot_nt(k_ref[0:left, sl], qc))
            raw.append(parts)
    comps = []
    for parts in raw:
        m = functools.reduce(jnp.maximum, [jnp.max(s, axis=0, keepdims=True) for s in parts])
        parts = [jnp.exp2(s - m) for s in parts]
        l = functools.reduce(jnp.add, [jnp.sum(p, axis=0, keepdims=True) for p in parts])
        comps.append((parts, l))
    for h in range(n_heads):
        rows = slice(h * A_VDIM, (h + 1) * A_VDIM)
        outs = []
        for parts, l in (comps[2 * h], comps[2 * h + 1]):
            acc = _dot(vt_ref[rows, left:left + tq], parts[0].astype(BF16))
            if c:
                acc = acc + _dot(vt_ref[rows, 0:left], parts[1].astype(BF16))
            outs.append(acc / l)
        o_t = outs[0] - lam * outs[1]
        ms = jnp.mean(o_t * o_t, axis=0, keepdims=True)
        o_t = o_t * lax.rsqrt(ms + EPS) * sub_ref[...] * (1.0 - lambda_init)
        o_ref[left:left + tq, h * LANES:(h + 1) * LANES] = o_t.T.astype(o_ref.dtype)


def _diff_attn_kernel(q_ref, k_ref, vt_ref, lam_ref, sub_ref, o_ref, *, lambda_init, tq):
    lp = lam_ref[...]
    lam = (jnp.exp(jnp.sum(lp[0:1] * lp[1:2], axis=-1, keepdims=True))
           - jnp.exp(jnp.sum(lp[2:3] * lp[3:4], axis=-1, keepdims=True)) + lambda_init)
    for c in range(q_ref.shape[0] // tq):
        _diff_attn_tile(c, tq, q_ref, k_ref, vt_ref, lam, sub_ref, o_ref, lambda_init)


def _diff_attn_call(qk, avt, lam_p, subln, batch, seq, lambda_init):
    n = qk.shape[0]
    n_groups = A_HEADS // DIFF_HEADS
    width = DIFF_HEADS * A_VDIM
    sub_t = jnp.broadcast_to(subln.astype(F32)[:, None], (A_VDIM, DIFF_TQ))
    return pl.pallas_call(
        functools.partial(_diff_attn_kernel, lambda_init=lambda_init, tq=DIFF_TQ),
        grid=(batch, n_groups),
        in_specs=[
            pl.BlockSpec((seq, width), lambda b, g: (b, g)),
            pl.BlockSpec((seq, width), lambda b, g: (b, n_groups + g)),
            pl.BlockSpec((width, seq), lambda b, g: (g, b)),
            pl.BlockSpec(lam_p.shape, lambda b, g: (0, 0)),
            pl.BlockSpec(sub_t.shape, lambda b, g: (0, 0)),
        ],
        out_specs=pl.BlockSpec((seq, width), lambda b, g: (b, g)),
        out_shape=jax.ShapeDtypeStruct((n, A_HEADS * A_VDIM), BF16),
        compiler_params=_params("parallel", "parallel"),
        name="diff_attn",
    )(qk, qk, avt, lam_p, sub_t)


def _dsa_kernel(iq_ref, ik_ref, iwt_ref, q_ref, k_ref, vt_ref, o_ref,
                qh_ref, qm_ref, score_ref, coarse_ref, acc_ref, *, topk):
    i = pl.program_id(1)
    tq = q_ref.shape[0]
    kb_size = score_ref.shape[1]
    n_kb = i + 1
    kf = float(topk)

    lane = lax.broadcasted_iota(jnp.int32, (tq, LANES), 1)
    idx_per_vreg = LANES // IDX_DIM
    for h in range(IDX_HEADS):
        x = iq_ref[:, (h // idx_per_vreg) * LANES:(h // idx_per_vreg + 1) * LANES]
        qh_ref[h] = jnp.where((lane >> 5) == (h % idx_per_vreg), x, jnp.zeros_like(x))
    for h in range(B_HEADS):
        x = q_ref[:, (h // 2) * LANES:(h // 2 + 1) * LANES]
        qm_ref[h] = jnp.where((lane >> 6) == (h % 2), x, jnp.zeros_like(x))

    q_chunk = (i * tq + lax.broadcasted_iota(jnp.int32, (kb_size, tq), 1)) >> 6

    def visible(kb):
        k_chunk = (kb * kb_size + lax.broadcasted_iota(jnp.int32, (kb_size, tq), 0)) >> 6
        return k_chunk <= q_chunk

    def score_block(kb, carry):
        start = pl.multiple_of(kb * kb_size, kb_size)
        ikb = ik_ref[pl.ds(start, kb_size), :]
        dots = [_dot_nt(ikb, qh_ref[h]) for h in range(IDX_HEADS)]
        score = jnp.maximum(dots[0], 0.0) * iwt_ref[0:1, :]
        for h in range(1, IDX_HEADS):
            score = score + jnp.maximum(dots[h], 0.0) * iwt_ref[h:h + 1, :]
        score = jnp.where(visible(kb), score + 0.0, -jnp.inf)
        score_ref[kb] = score
        coarse_ref[kb] = score.astype(BF16)
        return carry

    lax.fori_loop(0, n_kb, score_block, 0)

    def count(pred):
        def block(kb, acc):
            for r in range(kb_size // 8):
                x = score_ref[kb, r * 8:(r + 1) * 8, :]
                acc = acc + jnp.where(pred(x), 1.0, 0.0)
            return acc
        acc = lax.fori_loop(0, n_kb, block, jnp.zeros((8, tq), F32))
        return jnp.sum(acc, axis=0, keepdims=True)

    pack = 16
    one_b = jnp.ones((pack, tq), BF16)
    zero_b = jnp.zeros((pack, tq), BF16)

    def count_coarse(pred):
        def block(kb, acc):
            for r in range(kb_size // pack):
                x = coarse_ref[kb, r * pack:(r + 1) * pack, :]
                acc = acc + jnp.where(pred(x), one_b, zero_b)
            return acc
        acc = lax.fori_loop(0, n_kb, block, zero_b)
        return jnp.sum(acc.astype(F32), axis=0, keepdims=True)

    def key_to_float(key):
        return pltpu.bitcast(key ^ ((key >> 31) & 0x7FFFFFFF), F32)

    def key16_to_key(k16):
        return lax.shift_left(k16, 16) | ((k16 >> 31) & 0xFFFF)

    neg_inf_key16 = -32641
    neg_inf_key = -2139095041
    top_key16 = 32767

    def coarse_step(it, t):
        cand = t + lax.shift_left(jnp.int32(1), 15 - it)
        c_f = key_to_float(key16_to_key(jnp.minimum(cand, top_key16)))
        c_b = jnp.broadcast_to(c_f, (pack, tq)).astype(BF16)
        cnt = count_coarse(lambda x: x >= c_b)
        return jnp.where((cand <= top_key16) & (cnt >= kf), cand, t)

    t16 = lax.fori_loop(0, 16, coarse_step, jnp.full((1, tq), neg_inf_key16, jnp.int32))

    def fine_step(it, t):
        cand = t + lax.shift_left(jnp.int32(1), 16 - it)
        c_f = key_to_float(cand)
        cnt = count(lambda x: x >= c_f)
        return jnp.where((cand > t) & (cnt >= kf), cand, t)

    thr_key = lax.fori_loop(0, 17, fine_step, jnp.maximum(key16_to_key(t16 - 1), neg_inf_key))
    thr = key_to_float(thr_key)
    need = kf - count(lambda x: x > thr)

    r_i = lax.broadcasted_iota(jnp.int32, (LANES, LANES), 0)
    c_i = lax.broadcasted_iota(jnp.int32, (LANES, LANES), 1)
    strict_lower = jnp.where(c_i < r_i, 1.0, 0.0).astype(BF16)

    def select_bias(kb, offset):
        vis = visible(kb)
        rows = []
        for j in range(kb_size // LANES):
            sl = slice(j * LANES, (j + 1) * LANES)
            sj = score_ref[kb, sl, :]
            eq = sj == thr
            eq_f = jnp.where(eq, 1.0, 0.0)
            before = _dot(strict_lower, eq_f.astype(BF16)) + offset
            sel = ((sj > thr) | (eq & (before < need))) & vis[sl]
            rows.append(jnp.where(sel, 0.0, NEG))
            offset = offset + jnp.sum(eq_f, axis=0, keepdims=True)
        return jnp.concatenate(rows, axis=0), offset

    acc_ref[...] = jnp.zeros(acc_ref.shape, F32)

    def attn_block(kb, carry):
        ms, ls, offset = carry
        start = pl.multiple_of(kb * kb_size, kb_size)
        bias, offset = select_bias(kb, offset)
        new_m, new_l = [], []
        scores = []
        for h in range(B_HEADS):
            k2 = k_ref[pl.ds(start, kb_size), (h // 2) * LANES:(h // 2 + 1) * LANES]
            s = _dot_nt(k2, qm_ref[h]) + bias
            scores.append(s)
            new_m.append(jnp.maximum(ms[h], jnp.max(s, axis=0, keepdims=True)))
        for h in range(B_HEADS):
            alpha = jnp.exp2(ms[h] - new_m[h])
            p = jnp.exp2(scores[h] - new_m[h])
            new_l.append(alpha * ls[h] + jnp.sum(p, axis=0, keepdims=True))
            pv = _dot(vt_ref[kb, h * HEAD_DIM:(h + 1) * HEAD_DIM, :], p.astype(BF16))
            acc_ref[h] = alpha * acc_ref[h] + pv
        return tuple(new_m), tuple(new_l), offset

    m0 = tuple(jnp.full((1, tq), NEG, F32) for _ in range(B_HEADS))
    l0 = tuple(jnp.zeros((1, tq), F32) for _ in range(B_HEADS))
    _, ls, _ = lax.fori_loop(0, n_kb, attn_block, (m0, l0, jnp.zeros((1, tq), F32)))
    o_t = jnp.concatenate([acc_ref[h] / ls[h] for h in range(B_HEADS)], axis=0)
    o_ref[...] = o_t.T.astype(o_ref.dtype)


def _dsa_call(qk, vt, idx, iwt, batch, seq, topk):
    n = qk.shape[0]
    nq = seq // DSA_TQ
    width = B_HEADS * HEAD_DIM
    assert vt.shape[2] == DSA_TQ, "key blocks must match the projection's token tile"
    return pl.pallas_call(
        functools.partial(_dsa_kernel, topk=topk),
        grid=(batch, nq),
        in_specs=[
            pl.BlockSpec((DSA_TQ, 2 * LANES), lambda b, i: (b * nq + i, 0)),
            pl.BlockSpec((seq, LANES), lambda b, i: (b, 2)),
            pl.BlockSpec((IDX_HEADS, DSA_TQ), lambda b, i: (0, b * nq + i)),
            pl.BlockSpec((DSA_TQ, width), lambda b, i: (b * nq + i, 2)),
            pl.BlockSpec((seq, width), lambda b, i: (b, 3)),
            pl.BlockSpec((nq, width, DSA_TQ), lambda b, i: (b, 0, 0)),
        ],
        out_specs=pl.BlockSpec((DSA_TQ, width), lambda b, i: (b * nq + i, 0)),
        out_shape=jax.ShapeDtypeStruct((n, width), BF16),
        scratch_shapes=[
            pltpu.VMEM((IDX_HEADS, DSA_TQ, LANES), BF16),
            pltpu.VMEM((B_HEADS, DSA_TQ, LANES), BF16),
            pltpu.VMEM((nq, DSA_TQ, DSA_TQ), F32),
            pltpu.VMEM((nq, DSA_TQ, DSA_TQ), BF16),
            pltpu.VMEM((B_HEADS, HEAD_DIM, DSA_TQ), F32),
        ],
        compiler_params=_params("parallel", "arbitrary"),
        name="dsa",
    )(idx, idx, iwt, qk, qk, vt)


def _band_attn_kernel(q_ref, k_ref, vt_ref, tbl_ref, o_ref):
    n_heads = tbl_ref.shape[0]
    tq = tbl_ref.shape[2]
    n_blk = tbl_ref.shape[1] // tq
    n_tiles = q_ref.shape[0] // tq
    lane = lax.broadcasted_iota(jnp.int32, (tq, LANES), 1)
    for c in range(n_tiles):
        blocks = [(d, c - (n_blk - 1) + d) for d in range(n_blk) if c - (n_blk - 1) + d >= 0]
        scores, ms = [], []
        for h in range(n_heads):
            sl = slice((h // 2) * LANES, (h // 2 + 1) * LANES)
            q2 = q_ref[c * tq:(c + 1) * tq, sl]
            qm = jnp.where((lane >> 6) == (h % 2), q2, jnp.zeros_like(q2))
            parts = [_dot_nt(k_ref[kb * tq:(kb + 1) * tq, sl], qm) + tbl_ref[h, d * tq:(d + 1) * tq, :]
                     for d, kb in blocks]
            scores.append(parts)
            ms.append(functools.reduce(jnp.maximum, [jnp.max(s, axis=0, keepdims=True) for s in parts]))
        outs = []
        for h in range(n_heads):
            ps = [jnp.exp2(s - ms[h]) for s in scores[h]]
            l = functools.reduce(jnp.add, [jnp.sum(p, axis=0, keepdims=True) for p in ps])
            rows = slice(h * HEAD_DIM, (h + 1) * HEAD_DIM)
            pv = functools.reduce(jnp.add, [_dot(vt_ref[kb, rows, :], p.astype(BF16))
                                            for (d, kb), p in zip(blocks, ps)])
            outs.append(pv / l)
        o_ref[c * tq:(c + 1) * tq, :] = jnp.concatenate(outs, axis=0).T.astype(o_ref.dtype)


def _band_attn_call(qk, vt, tbl, batch, seq):
    n = qk.shape[0]
    nq = seq // BAND_TQ
    n_groups = C_HEADS // BAND_HEADS
    width = BAND_HEADS * HEAD_DIM
    return pl.pallas_call(
        _band_attn_kernel,
        grid=(n_groups, batch),
        in_specs=[
            pl.BlockSpec((seq, width), lambda g, b: (b, g)),
            pl.BlockSpec((seq, width), lambda g, b: (b, n_groups + g)),
            pl.BlockSpec((nq, width, BAND_TQ), lambda g, b: (b, g, 0)),
            pl.BlockSpec((BAND_HEADS,) + tbl.shape[1:], lambda g, b: (g, 0, 0)),
        ],
        out_specs=pl.BlockSpec((seq, width), lambda g, b: (b, g)),
        out_shape=jax.ShapeDtypeStruct((n, D_MODEL), BF16),
        compiler_params=_params("parallel", "parallel"),
        name="band_attn",
    )(qk, qk, vt, tbl)


def _rope_tables(seq):
    def one(dim):
        inv = ROPE_THETA ** (-jnp.arange(0, dim, 2, dtype=F32) / dim)
        ang = jnp.arange(seq, dtype=F32)[:, None] * inv[None, :]
        cos, sin = jnp.cos(ang), jnp.sin(ang)
        reps = LANES // dim
        return (jnp.tile(jnp.concatenate([cos, cos], axis=-1), (1, reps)),
                jnp.tile(jnp.concatenate([-sin, sin], axis=-1), (1, reps)))
    c64, s64 = one(HEAD_DIM)
    c32, s32 = one(IDX_DIM)
    return jnp.stack([c64, s64, c32, s32])


def _even_w_in_layout(w):
    d = w.shape[0]
    n_qkv = 6 * PROJ_CC
    iq = w[:, n_qkv:n_qkv + IDX_HEADS * IDX_DIM]
    ik = w[:, n_qkv + IDX_HEADS * IDX_DIM:n_qkv + IDX_HEADS * IDX_DIM + IDX_DIM]
    iw = w[:, n_qkv + IDX_HEADS * IDX_DIM + IDX_DIM:]
    return jnp.concatenate(
        [w[:, :n_qkv], iq, jnp.tile(ik, (1, LANES // IDX_DIM)),
         iw, jnp.zeros((d, LANES - IDX_HEADS), w.dtype)], axis=1).astype(BF16)


def _band_table_kernel(g_ref, o_ref):
    window, tq = o_ref.shape[1], o_ref.shape[2]
    span = g_ref.shape[2]
    gb = jnp.broadcast_to(g_ref[0], (window, span))
    tbl = pltpu.roll(gb, span - window, 1, stride=1, stride_axis=0)[:, :tq]
    kc = lax.broadcasted_iota(jnp.int32, (window, tq), 0) >> 6
    qc = (window - tq + lax.broadcasted_iota(jnp.int32, (window, tq), 1)) >> 6
    in_band = (kc <= qc) & (kc >= qc - C_LEFT_CHUNKS)
    o_ref[0] = jnp.where(in_band, tbl * LOG2E, NEG)


def _band_bias_table(rel_bias):
    h = rel_bias.shape[0]
    left = C_LEFT_CHUNKS * CHUNK
    window = left + BAND_TQ
    span = BAND_TQ + window
    n_near = window - left + 1 - CHUNK
    n_far = span - n_near - REL_TABLE
    rel = rel_bias.astype(F32)
    g = jnp.concatenate([jnp.broadcast_to(rel[:, :1], (h, n_near)), rel,
                         jnp.broadcast_to(rel[:, REL_TABLE - 1:], (h, n_far))], axis=1)
    return pl.pallas_call(
        _band_table_kernel,
        grid=(h,),
        in_specs=[pl.BlockSpec((1, 1, span), lambda i: (i, 0, 0))],
        out_specs=pl.BlockSpec((1, window, BAND_TQ), lambda i: (i, 0, 0)),
        out_shape=jax.ShapeDtypeStruct((h, window, BAND_TQ), F32),
        compiler_params=_params("parallel"),
        name="band_table",
    )(g[:, None, :])


def kernel(x, norm_g, ffn_wg, ffn_wu, ffn_wd, even_w_in, even_w_out, even_lambda, even_subln,
           odd_w_in, odd_w_out, odd_rel_bias):
    batch, seq, d = x.shape
    depth = norm_g.shape[0]
    topk = min(TOPK_MAX, seq // 4)
    x2d = x.reshape(batch * seq, d)
    tabs = _rope_tables(seq)
    ffn_w = (ffn_wg.astype(BF16), ffn_wu.astype(BF16), ffn_wd.astype(BF16))
    pending = ((), None)
    g_mix = norm_g[0, 3]
    for l in range(depth):
        g = norm_g[l]
        x2d = _ffn_call(x2d, pending[0], pending[1], jnp.stack([g_mix, g[0], g[1]]), *ffn_w, (l, 0))
        if l % 2 == 0:
            e = l // 2
            lambda_init = 0.8 - 0.6 * math.exp(-0.3 * l)
            qk, avt, bvt, idx, iwt = _proj_even_call(x2d, g[2:3], _even_w_in_layout(even_w_in[e]), tabs, seq)
            o_a = _diff_attn_call(qk, avt, even_lambda[e], even_subln[e], batch, seq, lambda_init)
            o_b = _dsa_call(qk, bvt, idx, iwt, batch, seq, topk)
            pending = ((o_a, o_b), even_w_out[e].astype(BF16))
        else:
            o = l // 2
            qk, vt = _proj_odd_call(x2d, g[2:3], odd_w_in[o].astype(BF16))
            o_c = _band_attn_call(qk, vt, _band_bias_table(odd_rel_bias[o]), batch, seq)
            pending = ((o_c,), odd_w_out[o].astype(BF16))
        g_mix = g[3]
        x2d = _ffn_call(x2d, pending[0], pending[1], jnp.stack([g_mix, g[4], g[5]]), *ffn_w, (l, 1))
        pending = ((), None)
    return x2d.reshape(batch, seq, d)
```

```python
import functools
import math

import jax
import jax.numpy as jnp
from jax import lax
from jax.experimental import pallas as pl
from jax.experimental.pallas import tpu as pltpu

D_MODEL = 1024
CHUNK = 64
HEAD_DIM = 64
ROPE_THETA = 10000.0
EPS = 1e-6
D_FF = 2816
A_HEADS = D_MODEL // (4 * HEAD_DIM)
A_VDIM = 2 * HEAD_DIM
B_HEADS = D_MODEL // (2 * HEAD_DIM)
IDX_HEADS = 8
IDX_DIM = 32
TOPK_MAX = 256
C_HEADS = D_MODEL // HEAD_DIM
C_LEFT_CHUNKS = 8
MAX_REL = 256
REL_TABLE = MAX_REL + CHUNK

LANES = 128
LOG2E = math.log2(math.e)
NEG = -1e30
VMEM_LIMIT = 56 * 1024 * 1024

FFN_TM = 1024
FFN_FC = 256
PROJ_TM = 512
PROJ_CC = 512
DIFF_TQ = 256
DIFF_HEADS = 2
DSA_TQ = 512
BAND_TQ = 256
BAND_HEADS = 8

BF16 = jnp.bfloat16
F32 = jnp.float32


def _dot(a, b):
    return jnp.dot(a, b, preferred_element_type=F32)


def _dot_nt(a, b):
    return lax.dot_general(a, b, (((1,), (1,)), ((), ())), preferred_element_type=F32)


def _rms(xf, g):
    ms = jnp.mean(xf * xf, axis=-1, keepdims=True)
    return xf * lax.rsqrt(ms + EPS) * g


def _params(*sem):
    return pltpu.CompilerParams(dimension_semantics=sem, vmem_limit_bytes=VMEM_LIMIT)


def _ffn_kernel(*refs, n_mix, n_chunks):
    x_ref = refs[0]
    mix_refs = refs[1:1 + n_mix]
    wout_refs = refs[1 + n_mix:1 + 2 * n_mix]
    g_ref, wg_ref, wu_ref, wd_ref, out_ref = refs[1 + 2 * n_mix:]

    x = x_ref[...]
    if n_mix:
        m = _dot(mix_refs[0][...], wout_refs[0][...])
        for a_ref, w_ref in zip(mix_refs[1:], wout_refs[1:]):
            m = m + _dot(a_ref[...], w_ref[...])
        x = x + _rms(m, g_ref[0:1, :])
    h = _rms(x, g_ref[1:2, :]).astype(BF16)
    acc = jnp.zeros(x.shape, F32)
    for c in range(n_chunks):
        cols = slice(c * FFN_FC, (c + 1) * FFN_FC)
        a = _dot(h, wg_ref[:, cols])
        b = _dot(h, wu_ref[:, cols])
        t = (a / (1.0 + jnp.exp(-a))) * b
        acc = acc + _dot(t.astype(BF16), wd_ref[cols, :])
    out_ref[...] = x + 0.5 * _rms(acc, g_ref[2:3, :])


def _ffn_call(x2d, mixes, w_out, g3, wg, wu, wd, which):
    n, d = x2d.shape
    nc = D_FF // FFN_FC
    n_mix = len(mixes)
    row = lambda i: (i, 0)
    const2 = lambda i: (0, 0)
    in_specs = [pl.BlockSpec((FFN_TM, d), row)]
    in_specs += [pl.BlockSpec((FFN_TM, m.shape[1]), row) for m in mixes]
    once = pl.Buffered(1)
    row0 = 0
    for m in mixes:
        in_specs.append(pl.BlockSpec((m.shape[1], d), lambda i, r=row0 // m.shape[1]: (r, 0),
                                     pipeline_mode=once))
        row0 += m.shape[1]
    in_specs += [
        pl.BlockSpec(g3.shape, const2),
        pl.BlockSpec((None, None) + wg.shape[2:], lambda i: which + (0, 0), pipeline_mode=once),
        pl.BlockSpec((None, None) + wu.shape[2:], lambda i: which + (0, 0), pipeline_mode=once),
        pl.BlockSpec((None, None) + wd.shape[2:], lambda i: which + (0, 0), pipeline_mode=once),
    ]
    return pl.pallas_call(
        functools.partial(_ffn_kernel, n_mix=n_mix, n_chunks=nc),
        grid=(n // FFN_TM,),
        in_specs=in_specs,
        out_specs=pl.BlockSpec((FFN_TM, d), row),
        out_shape=jax.ShapeDtypeStruct((n, d), F32),
        compiler_params=_params("parallel"),
        name="ffn_mix" if n_mix else "ffn",
    )(x2d, *mixes, *([w_out] * n_mix), g3, wg, wu, wd)


def _rope(p, cos_t, sin_t, half):
    lane = lax.broadcasted_iota(jnp.int32, p.shape, 1)
    first = (lane & (2 * half - 1)) < half
    partner = jnp.where(first, pltpu.roll(p, LANES - half, 1), pltpu.roll(p, half, 1))
    return p * cos_t + partner * sin_t


def _proj_even_kernel(x_ref, g_ref, w_ref, tab_ref, qk_ref, avt_ref, bvt_ref, idx_ref, iwt_ref):
    h = _rms(x_ref[...], g_ref[0:1, :]).astype(BF16)
    c64, s64, c32, s32 = tab_ref[0], tab_ref[1], tab_ref[2], tab_ref[3]
    q_scale = HEAD_DIM ** -0.5 * LOG2E
    out_col = 0
    for c in (0, 1, 3, 4):
        p = _dot(h, w_ref[:, c * PROJ_CC:(c + 1) * PROJ_CC])
        for j in range(PROJ_CC // LANES):
            pj = _rope(p[:, j * LANES:(j + 1) * LANES], c64, s64, HEAD_DIM // 2)
            if c in (0, 3):
                pj = pj * q_scale
            qk_ref[:, out_col:out_col + LANES] = pj.astype(BF16)
            out_col += LANES
    avt_ref[...] = _dot(h, w_ref[:, 2 * PROJ_CC:3 * PROJ_CC]).T.astype(BF16)
    bvt_ref[0] = _dot(h, w_ref[:, 5 * PROJ_CC:6 * PROJ_CC]).T.astype(BF16)
    p = _dot(h, w_ref[:, 6 * PROJ_CC:7 * PROJ_CC])
    for j in range(3):
        pj = _rope(p[:, j * LANES:(j + 1) * LANES], c32, s32, IDX_DIM // 2)
        idx_ref[:, j * LANES:(j + 1) * LANES] = pj.astype(BF16)
    iw = p[:, 3 * LANES:4 * LANES] * (IDX_HEADS ** -0.5 * IDX_DIM ** -0.5)
    iwt_ref[...] = iw.T[0:IDX_HEADS, :]


def _proj_even_call(x2d, g, w, tabs, seq):
    n, d = x2d.shape
    steps_per_seq = seq // PROJ_TM
    return pl.pallas_call(
        _proj_even_kernel,
        grid=(n // PROJ_TM,),
        in_specs=[
            pl.BlockSpec((PROJ_TM, d), lambda i: (i, 0)),
            pl.BlockSpec(g.shape, lambda i: (0, 0)),
            pl.BlockSpec(w.shape, lambda i: (0, 0)),
            pl.BlockSpec((4, PROJ_TM, LANES), lambda i: (0, i % steps_per_seq, 0)),
        ],
        out_specs=[
            pl.BlockSpec((PROJ_TM, 4 * PROJ_CC), lambda i: (i, 0)),
            pl.BlockSpec((PROJ_CC, PROJ_TM), lambda i: (0, i)),
            pl.BlockSpec((1, PROJ_CC, PROJ_TM), lambda i: (i, 0, 0)),
            pl.BlockSpec((PROJ_TM, 3 * LANES), lambda i: (i, 0)),
            pl.BlockSpec((IDX_HEADS, PROJ_TM), lambda i: (0, i)),
        ],
        out_shape=[
            jax.ShapeDtypeStruct((n, 4 * PROJ_CC), BF16),
            jax.ShapeDtypeStruct((PROJ_CC, n), BF16),
            jax.ShapeDtypeStruct((n // PROJ_TM, PROJ_CC, PROJ_TM), BF16),
            jax.ShapeDtypeStruct((n, 3 * LANES), BF16),
            jax.ShapeDtypeStruct((IDX_HEADS, n), F32),
        ],
        compiler_params=_params("parallel"),
        name="proj_even",
    )(x2d, g, w, tabs)


def _proj_odd_kernel(x_ref, g_ref, w_ref, qk_ref, vt_ref):
    h = _rms(x_ref[...], g_ref[0:1, :]).astype(BF16)
    q_scale = HEAD_DIM ** -0.5 * LOG2E
    for c in range(2 * D_MODEL // PROJ_CC):
        p = _dot(h, w_ref[:, c * PROJ_CC:(c + 1) * PROJ_CC])
        if c * PROJ_CC < D_MODEL:
            p = p * q_scale
        qk_ref[:, c * PROJ_CC:(c + 1) * PROJ_CC] = p.astype(BF16)
    for c in range(D_MODEL // PROJ_CC):
        col = 2 * D_MODEL + c * PROJ_CC
        vt = _dot(h, w_ref[:, col:col + PROJ_CC]).T.astype(BF16)
        for j in range(vt_ref.shape[0]):
            vt_ref[j, c * PROJ_CC:(c + 1) * PROJ_CC, :] = vt[:, j * BAND_TQ:(j + 1) * BAND_TQ]


def _proj_odd_call(x2d, g, w):
    n, d = x2d.shape
    return pl.pallas_call(
        _proj_odd_kernel,
        grid=(n // PROJ_TM,),
        in_specs=[
            pl.BlockSpec((PROJ_TM, d), lambda i: (i, 0)),
            pl.BlockSpec(g.shape, lambda i: (0, 0)),
            pl.BlockSpec(w.shape, lambda i: (0, 0)),
        ],
        out_specs=[
            pl.BlockSpec((PROJ_TM, 2 * d), lambda i: (i, 0)),
            pl.BlockSpec((PROJ_TM // BAND_TQ, d, BAND_TQ), lambda i: (i, 0, 0)),
        ],
        out_shape=[
            jax.ShapeDtypeStruct((n, 2 * d), BF16),
            jax.ShapeDtypeStruct((n // BAND_TQ, d, BAND_TQ), BF16),
        ],
        compiler_params=_params("parallel"),
        name="proj_odd",
    )(x2d, g, w)


def _diff_attn_tile(c, tq, q_ref, k_ref, vt_ref, lam, sub_ref, o_ref, lambda_init):
    n_heads = q_ref.shape[1] // LANES
    left = c * tq
    lane = lax.broadcasted_iota(jnp.int32, (tq, LANES), 1)
    key_chunk = lax.broadcasted_iota(jnp.int32, (tq, tq), 0) >> 6
    qry_chunk = lax.broadcasted_iota(jnp.int32, (tq, tq), 1) >> 6
    diag_mask = key_chunk <= qry_chunk
    raw = []
    for h in range(n_heads):
        sl = slice(h * LANES, (h + 1) * LANES)
        q = q_ref[left:left + tq, sl]
        for keep in (lane < HEAD_DIM, lane >= HEAD_DIM):
            qc = jnp.where(keep, q, jnp.zeros_like(q))
            parts = [jnp.where(diag_mask, _dot_nt(k_ref[left:left + tq, sl], qc), NEG)]
            if c:
                parts.append(_dot_nt(k_ref[0:left, sl], qc))
            raw.append(parts)
    comps = []
    for parts in raw:
        m = functools.reduce(jnp.maximum, [jnp.max(s, axis=0, keepdims=True) for s in parts])
        parts = [jnp.exp2(s - m) for s in parts]
        l = functools.reduce(jnp.add, [jnp.sum(p, axis=0, keepdims=True) for p in parts])
        comps.append((parts, l))
    for h in range(n_heads):
        rows = slice(h * A_VDIM, (h + 1) * A_VDIM)
        outs = []
        for parts, l in (comps[2 * h], comps[2 * h + 1]):
            acc = _dot(vt_ref[rows, left:left + tq], parts[0].astype(BF16))
            if c:
                acc = acc + _dot(vt_ref[rows, 0:left], parts[1].astype(BF16))
            outs.append(acc / l)
        o_t = outs[0] - lam * outs[1]
        ms = jnp.mean(o_t * o_t, axis=0, keepdims=True)
        o_t = o_t * lax.rsqrt(ms + EPS) * sub_ref[...] * (1.0 - lambda_init)
        o_ref[left:left + tq, h * LANES:(h + 1) * LANES] = o_t.T.astype(o_ref.dtype)


def _diff_attn_kernel(q_ref, k_ref, vt_ref, lam_ref, sub_ref, o_ref, *, lambda_init, tq):
    lp = lam_ref[...]
    lam = (jnp.exp(jnp.sum(lp[0:1] * lp[1:2], axis=-1, keepdims=True))
           - jnp.exp(jnp.sum(lp[2:3] * lp[3:4], axis=-1, keepdims=True)) + lambda_init)
    for c in range(q_ref.shape[0] // tq):
        _diff_attn_tile(c, tq, q_ref, k_ref, vt_ref, lam, sub_ref, o_ref, lambda_init)


def _diff_attn_call(qk, avt, lam_p, subln, batch, seq, lambda_init):
    n = qk.shape[0]
    n_groups = A_HEADS // DIFF_HEADS
    width = DIFF_HEADS * A_VDIM
    sub_t = jnp.broadcast_to(subln.astype(F32)[:, None], (A_VDIM, DIFF_TQ))
    return pl.pallas_call(
        functools.partial(_diff_attn_kernel, lambda_init=lambda_init, tq=DIFF_TQ),
        grid=(batch, n_groups),
        in_specs=[
            pl.BlockSpec((seq, width), lambda b, g: (b, g)),
            pl.BlockSpec((seq, width), lambda b, g: (b, n_groups + g)),
            pl.BlockSpec((width, seq), lambda b, g: (g, b)),
            pl.BlockSpec(lam_p.shape, lambda b, g: (0, 0)),
            pl.BlockSpec(sub_t.shape, lambda b, g: (0, 0)),
        ],
        out_specs=pl.BlockSpec((seq, width), lambda b, g: (b, g)),
        out_shape=jax.ShapeDtypeStruct((n, A_HEADS * A_VDIM), BF16),
        compiler_params=_params("parallel", "parallel"),
        name="diff_attn",
    )(qk, qk, avt, lam_p, sub_t)


def _dsa_kernel(iq_ref, ik_ref, iwt_ref, q_ref, k_ref, vt_ref, o_ref,
                qh_ref, qm_ref, score_ref, coarse_ref, acc_ref, *, topk):
    i = pl.program_id(1)
    tq = q_ref.shape[0]
    kb_size = score_ref.shape[1]
    n_kb = i + 1
    kf = float(topk)

    lane = lax.broadcasted_iota(jnp.int32, (tq, LANES), 1)
    idx_per_vreg = LANES // IDX_DIM
    for h in range(IDX_HEADS):
        x = iq_ref[:, (h // idx_per_vreg) * LANES:(h // idx_per_vreg + 1) * LANES]
        qh_ref[h] = jnp.where((lane >> 5) == (h % idx_per_vreg), x, jnp.zeros_like(x))
    for h in range(B_HEADS):
        x = q_ref[:, (h // 2) * LANES:(h // 2 + 1) * LANES]
        qm_ref[h] = jnp.where((lane >> 6) == (h % 2), x, jnp.zeros_like(x))

    q_chunk = (i * tq + lax.broadcasted_iota(jnp.int32, (kb_size, tq), 1)) >> 6

    def visible(kb):
        k_chunk = (kb * kb_size + lax.broadcasted_iota(jnp.int32, (kb_size, tq), 0)) >> 6
        return k_chunk <= q_chunk

    def score_block(kb, carry):
        start = pl.multiple_of(kb * kb_size, kb_size)
        ikb = ik_ref[pl.ds(start, kb_size), :]
        dots = [_dot_nt(ikb, qh_ref[h]) for h in range(IDX_HEADS)]
        score = jnp.maximum(dots[0], 0.0) * iwt_ref[0:1, :]
        for h in range(1, IDX_HEADS):
            score = score + jnp.maximum(dots[h], 0.0) * iwt_ref[h:h + 1, :]
        score = jnp.where(visible(kb), score + 0.0, -jnp.inf)
        score_ref[kb] = score
        coarse_ref[kb] = score.astype(BF16)
        return carry

    lax.fori_loop(0, n_kb, score_block, 0)

    def count(pred):
        def block(kb, acc):
            for r in range(kb_size // 8):
                x = score_ref[kb, r * 8:(r + 1) * 8, :]
                acc = acc + jnp.where(pred(x), 1.0, 0.0)
            return acc
        acc = lax.fori_loop(0, n_kb, block, jnp.zeros((8, tq), F32))
        return jnp.sum(acc, axis=0, keepdims=True)

    pack = 16
    one_b = jnp.ones((pack, tq), BF16)
    zero_b = jnp.zeros((pack, tq), BF16)

    def count_coarse(pred):
        def block(kb, acc):
            for r in range(kb_size // pack):
                x = coarse_ref[kb, r * pack:(r + 1) * pack, :]
                acc = acc + jnp.where(pred(x), one_b, zero_b)
            return acc
        acc = lax.fori_loop(0, n_kb, block, zero_b)
        return jnp.sum(acc.astype(F32), axis=0, keepdims=True)

    def key_to_float(key):
        return pltpu.bitcast(key ^ ((key >> 31) & 0x7FFFFFFF), F32)

    def key16_to_key(k16):
        return lax.shift_left(k16, 16) | ((k16 >> 31) & 0xFFFF)

    neg_inf_key16 = -32641
    neg_inf_key = -2139095041
    top_key16 = 32767

    def coarse_step(it, t):
        cand = t + lax.shift_left(jnp.int32(1), 15 - it)
        c_f = key_to_float(key16_to_key(jnp.minimum(cand, top_key16)))
        c_b = jnp.broadcast_to(c_f, (pack, tq)).astype(BF16)
        cnt = count_coarse(lambda x: x >= c_b)
        return jnp.where((cand <= top_key16) & (cnt >= kf), cand, t)

    t16 = lax.fori_loop(0, 16, coarse_step, jnp.full((1, tq), neg_inf_key16, jnp.int32))

    def fine_step(it, t):
        cand = t + lax.shift_left(jnp.int32(1), 16 - it)
        c_f = key_to_float(cand)
        cnt = count(lambda x: x >= c_f)
        return jnp.where((cand > t) & (cnt >= kf), cand, t)

    thr_key = lax.fori_loop(0, 17, fine_step, jnp.maximum(key16_to_key(t16 - 1), neg_inf_key))
    thr = key_to_float(thr_key)
    need = kf - count(lambda x: x > thr)

    r_i = lax.broadcasted_iota(jnp.int32, (LANES, LANES), 0)
    c_i = lax.broadcasted_iota(jnp.int32, (LANES, LANES), 1)
    strict_lower = jnp.where(c_i < r_i, 1.0, 0.0).astype(BF16)

    def select_bias(kb, offset):
        vis = visible(kb)
        rows = []
        for j in range(kb_size // LANES):
            sl = slice(j * LANES, (j + 1) * LANES)
            sj = score_ref[kb, sl, :]
            eq = sj == thr
            eq_f = jnp.where(eq, 1.0, 0.0)
            before = _dot(strict_lower, eq_f.astype(BF16)) + offset
            sel = ((sj > thr) | (eq & (before < need))) & vis[sl]
            rows.append(jnp.where(sel, 0.0, NEG))
            offset = offset + jnp.sum(eq_f, axis=0, keepdims=True)
        return jnp.concatenate(rows, axis=0), offset

    acc_ref[...] = jnp.zeros(acc_ref.shape, F32)

    def attn_block(kb, carry):
        ms, ls, offset = carry
        start = pl.multiple_of(kb * kb_size, kb_size)
        bias, offset = select_bias(kb, offset)
        new_m, new_l = [], []
        scores = []
        for h in range(B_HEADS):
            k2 = k_ref[pl.ds(start, kb_size), (h // 2) * LANES:(h // 2 + 1) * LANES]
            s = _dot_nt(k2, qm_ref[h]) + bias
            scores.append(s)
            new_m.append(jnp.maximum(ms[h], jnp.max(s, axis=0, keepdims=True)))
        for h in range(B_HEADS):
            alpha = jnp.exp2(ms[h] - new_m[h])
            p = jnp.exp2(scores[h] - new_m[h])
            new_l.append(alpha * ls[h] + jnp.sum(p, axis=0, keepdims=True))
            pv = _dot(vt_ref[kb, h * HEAD_DIM:(h + 1) * HEAD_DIM, :], p.astype(BF16))
            acc_ref[h] = alpha * acc_ref[h] + pv
        return tuple(new_m), tuple(new_l), offset

    m0 = tuple(jnp.full((1, tq), NEG, F32) for _ in range(B_HEADS))
    l0 = tuple(jnp.zeros((1, tq), F32) for _ in range(B_HEADS))
    _, ls, _ = lax.fori_loop(0, n_kb, attn_block, (m0, l0, jnp.zeros((1, tq), F32)))
    o_t = jnp.concatenate([acc_ref[h] / ls[h] for h in range(B_HEADS)], axis=0)
    o_ref[...] = o_t.T.astype(o_ref.dtype)


def _dsa_call(qk, vt, idx, iwt, batch, seq, topk):
    n = qk.shape[0]
    nq = seq // DSA_TQ
    width = B_HEADS * HEAD_DIM
    assert vt.shape[2] == DSA_TQ, "key blocks must match the projection's token tile"
    return pl.pallas_call(
        functools.partial(_dsa_kernel, topk=topk),
        grid=(batch, nq),
        in_specs=[
            pl.BlockSpec((DSA_TQ, 2 * LANES), lambda b, i: (b * nq + i, 0)),
            pl.BlockSpec((seq, LANES), lambda b, i: (b, 2)),
            pl.BlockSpec((IDX_HEADS, DSA_TQ), lambda b, i: (0, b * nq + i)),
            pl.BlockSpec((DSA_TQ, width), lambda b, i: (b * nq + i, 2)),
            pl.BlockSpec((seq, width), lambda b, i: (b, 3)),
            pl.BlockSpec((nq, width, DSA_TQ), lambda b, i: (b, 0, 0)),
        ],
        out_specs=pl.BlockSpec((DSA_TQ, width), lambda b, i: (b * nq + i, 0)),
        out_shape=jax.ShapeDtypeStruct((n, width), BF16),
        scratch_shapes=[
            pltpu.VMEM((IDX_HEADS, DSA_TQ, LANES), BF16),
            pltpu.VMEM((B_HEADS, DSA_TQ, LANES), BF16),
            pltpu.VMEM((nq, DSA_TQ, DSA_TQ), F32),
            pltpu.VMEM((nq, DSA_TQ, DSA_TQ), BF16),
            pltpu.VMEM((B_HEADS, HEAD_DIM, DSA_TQ), F32),
        ],
        compiler_params=_params("parallel", "arbitrary"),
        name="dsa",
    )(idx, idx, iwt, qk, qk, vt)


def _band_attn_kernel(q_ref, k_ref, vt_ref, tbl_ref, o_ref):
    n_heads = tbl_ref.shape[0]
    tq = tbl_ref.shape[2]
    n_blk = tbl_ref.shape[1] // tq
    n_tiles = q_ref.shape[0] // tq
    lane = lax.broadcasted_iota(jnp.int32, (tq, LANES), 1)
    for c in range(n_tiles):
        blocks = [(d, c - (n_blk - 1) + d) for d in range(n_blk) if c - (n_blk - 1) + d >= 0]
        scores, ms = [], []
        for h in range(n_heads):
            sl = slice((h // 2) * LANES, (h // 2 + 1) * LANES)
            q2 = q_ref[c * tq:(c + 1) * tq, sl]
            qm = jnp.where((lane >> 6) == (h % 2), q2, jnp.zeros_like(q2))
            parts = [_dot_nt(k_ref[kb * tq:(kb + 1) * tq, sl], qm) + tbl_ref[h, d * tq:(d + 1) * tq, :]
                     for d, kb in blocks]
            scores.append(parts)
            ms.append(functools.reduce(jnp.maximum, [jnp.max(s, axis=0, keepdims=True) for s in parts]))
        outs = []
        for h in range(n_heads):
            ps = [jnp.exp2(s - ms[h]) for s in scores[h]]
            l = functools.reduce(jnp.add, [jnp.sum(p, axis=0, keepdims=True) for p in ps])
            rows = slice(h * HEAD_DIM, (h + 1) * HEAD_DIM)
            pv = functools.reduce(jnp.add, [_dot(vt_ref[kb, rows, :], p.astype(BF16))
                                            for (d, kb), p in zip(blocks, ps)])
            outs.append(pv / l)
        o_ref[c * tq:(c + 1) * tq, :] = jnp.concatenate(outs, axis=0).T.astype(o_ref.dtype)


def _band_attn_call(qk, vt, tbl, batch, seq):
    n = qk.shape[0]
    nq = seq // BAND_TQ
    n_groups = C_HEADS // BAND_HEADS
    width = BAND_HEADS * HEAD_DIM
    return pl.pallas_call(
        _band_attn_kernel,
        grid=(n_groups, batch),
        in_specs=[
            pl.BlockSpec((seq, width), lambda g, b: (b, g)),
            pl.BlockSpec((seq, width), lambda g, b: (b, n_groups + g)),
            pl.BlockSpec((nq, width, BAND_TQ), lambda g, b: (b, g, 0)),
            pl.BlockSpec((BAND_HEADS,) + tbl.shape[1:], lambda g, b: (g, 0, 0)),
        ],
        out_specs=pl.BlockSpec((seq, width), lambda g, b: (b, g)),
        out_shape=jax.ShapeDtypeStruct((n, D_MODEL), BF16),
        compiler_params=_params("parallel", "parallel"),
        name="band_attn",
    )(qk, qk, vt, tbl)


def _rope_tables(seq):
    def one(dim):
        inv = ROPE_THETA ** (-jnp.arange(0, dim, 2, dtype=F32) / dim)
        ang = jnp.arange(seq, dtype=F32)[:, None] * inv[None, :]
        cos, sin = jnp.cos(ang), jnp.sin(ang)
        reps = LANES // dim
        return (jnp.tile(jnp.concatenate([cos, cos], axis=-1), (1, reps)),
                jnp.tile(jnp.concatenate([-sin, sin], axis=-1), (1, reps)))
    c64, s64 = one(HEAD_DIM)
    c32, s32 = one(IDX_DIM)
    return jnp.stack([c64, s64, c32, s32])


def _even_w_in_layout(w):
    d = w.shape[0]
    n_qkv = 6 * PROJ_CC
    iq = w[:, n_qkv:n_qkv + IDX_HEADS * IDX_DIM]
    ik = w[:, n_qkv + IDX_HEADS * IDX_DIM:n_qkv + IDX_HEADS * IDX_DIM + IDX_DIM]
    iw = w[:, n_qkv + IDX_HEADS * IDX_DIM + IDX_DIM:]
    return jnp.concatenate(
        [w[:, :n_qkv], iq, jnp.tile(ik, (1, LANES // IDX_DIM)),
         iw, jnp.zeros((d, LANES - IDX_HEADS), w.dtype)], axis=1).astype(BF16)


def _band_table_kernel(g_ref, o_ref):
    window, tq = o_ref.shape[1], o_ref.shape[2]
    span = g_ref.shape[2]
    gb = jnp.broadcast_to(g_ref[0], (window, span))
    tbl = pltpu.roll(gb, span - window, 1, stride=1, stride_axis=0)[:, :tq]
    kc = lax.broadcasted_iota(jnp.int32, (window, tq), 0) >> 6
    qc = (window - tq + lax.broadcasted_iota(jnp.int32, (window, tq), 1)) >> 6
    in_band = (kc <= qc) & (kc >= qc - C_LEFT_CHUNKS)
    o_ref[0] = jnp.where(in_band, tbl * LOG2E, NEG)


def _band_bias_table(rel_bias):
    h = rel_bias.shape[0]
    left = C_LEFT_CHUNKS * CHUNK
    window = left + BAND_TQ
    span = BAND_TQ + window
    n_near = window - left + 1 - CHUNK
    n_far = span - n_near - REL_TABLE
    rel = rel_bias.astype(F32)
    g = jnp.concatenate([jnp.broadcast_to(rel[:, :1], (h, n_near)), rel,
                         jnp.broadcast_to(rel[:, REL_TABLE - 1:], (h, n_far))], axis=1)
    return pl.pallas_call(
        _band_table_kernel,
        grid=(h,),
        in_specs=[pl.BlockSpec((1, 1, span), lambda i: (i, 0, 0))],
        out_specs=pl.BlockSpec((1, window, BAND_TQ), lambda i: (i, 0, 0)),
        out_shape=jax.ShapeDtypeStruct((h, window, BAND_TQ), F32),
        compiler_params=_params("parallel"),
        name="band_table",
    )(g[:, None, :])


def kernel(x, norm_g, ffn_wg, ffn_wu, ffn_wd, even_w_in, even_w_out, even_lambda, even_subln,
           odd_w_in, odd_w_out, odd_rel_bias):
    batch, seq, d = x.shape
    depth = norm_g.shape[0]
    topk = min(TOPK_MAX, seq // 4)
    x2d = x.reshape(batch * seq, d)
    tabs = _rope_tables(seq)
    ffn_w = (ffn_wg.astype(BF16), ffn_wu.astype(BF16), ffn_wd.astype(BF16))
    pending = ((), None)
    g_mix = norm_g[0, 3]
    for l in range(depth):
        g = norm_g[l]
        x2d = _ffn_call(x2d, pending[0], pending[1], jnp.stack([g_mix, g[0], g[1]]), *ffn_w, (l, 0))
        if l % 2 == 0:
            e = l // 2
            lambda_init = 0.8 - 0.6 * math.exp(-0.3 * l)
            qk, avt, bvt, idx, iwt = _proj_even_call(x2d, g[2:3], _even_w_in_layout(even_w_in[e]), tabs, seq)
            o_a = _diff_attn_call(qk, avt, even_lambda[e], even_subln[e], batch, seq, lambda_init)
            o_b = _dsa_call(qk, bvt, idx, iwt, batch, seq, topk)
            pending = ((o_a, o_b), even_w_out[e].astype(BF16))
        else:
            o = l // 2
            qk, vt = _proj_odd_call(x2d, g[2:3], odd_w_in[o].astype(BF16))
            o_c = _band_attn_call(qk, vt, _band_bias_table(odd_rel_bias[o]), batch, seq)
            pending = ((o_c,), odd_w_out[o].astype(BF16))
        g_mix = g[3]
        x2d = _ffn_call(x2d, pending[0], pending[1], jnp.stack([g_mix, g[4], g[5]]), *ffn_w, (l, 1))
        pending = ((), None)
    return x2d.reshape(batch, seq, d)
```

```python
import functools
import math

import jax
import jax.numpy as jnp
from jax import lax
from jax.experimental import pallas as pl
from jax.experimental.pallas import tpu as pltpu

D_MODEL = 1024
CHUNK = 64
HEAD_DIM = 64
ROPE_THETA = 10000.0
EPS = 1e-6
D_FF = 2816
A_HEADS = D_MODEL // (4 * HEAD_DIM)
A_VDIM = 2 * HEAD_DIM
B_HEADS = D_MODEL // (2 * HEAD_DIM)
IDX_HEADS = 8
IDX_DIM = 32
TOPK_MAX = 256
C_HEADS = D_MODEL // HEAD_DIM
C_LEFT_CHUNKS = 8
MAX_REL = 256
REL_TABLE = MAX_REL + CHUNK

LANES = 128
LOG2E = math.log2(math.e)
NEG = -1e30
VMEM_LIMIT = 56 * 1024 * 1024

FFN_TM = 1024
FFN_SUB = 2
FFN_FC = 256
PROJ_TM = 512
PROJ_CC = 512
DIFF_TQ = 256
DIFF_HEADS = 2
DSA_TQ = 512
BAND_TQ = 256
BAND_HEADS = 4

BF16 = jnp.bfloat16
F32 = jnp.float32


def _dot(a, b):
    return jnp.dot(a, b, preferred_element_type=F32)


def _dot_nt(a, b):
    return lax.dot_general(a, b, (((1,), (1,)), ((), ())), preferred_element_type=F32)


def _rms(xf, g):
    ms = jnp.mean(xf * xf, axis=-1, keepdims=True)
    return xf * lax.rsqrt(ms + EPS) * g


def _params(*sem):
    return pltpu.CompilerParams(dimension_semantics=sem, vmem_limit_bytes=VMEM_LIMIT)


def _ffn_kernel(*refs, n_mix, n_chunks):
    x_ref = refs[0]
    mix_refs = refs[1:1 + n_mix]
    wout_refs = refs[1 + n_mix:1 + 2 * n_mix]
    g_ref, wg_ref, wu_ref, wd_ref, out_ref = refs[1 + 2 * n_mix:]

    sub = x_ref.shape[0] // FFN_SUB
    for r in range(FFN_SUB):
        rows = slice(r * sub, (r + 1) * sub)
        x = x_ref[rows, :]
        if n_mix:
            m = _dot(mix_refs[0][rows, :], wout_refs[0][...])
            for a_ref, w_ref in zip(mix_refs[1:], wout_refs[1:]):
                m = m + _dot(a_ref[rows, :], w_ref[...])
            x = x + _rms(m, g_ref[0:1, :])
        h = _rms(x, g_ref[1:2, :]).astype(BF16)
        acc = jnp.zeros(x.shape, F32)
        for c in range(n_chunks):
            cols = slice(c * FFN_FC, (c + 1) * FFN_FC)
            a = _dot(h, wg_ref[:, cols])
            b = _dot(h, wu_ref[:, cols])
            t = (a / (1.0 + jnp.exp(-a))) * b
            acc = acc + _dot(t.astype(BF16), wd_ref[cols, :])
        out_ref[rows, :] = x + 0.5 * _rms(acc, g_ref[2:3, :])


def _ffn_call(x2d, mixes, w_out, g3, wg, wu, wd, which):
    n, d = x2d.shape
    nc = D_FF // FFN_FC
    n_mix = len(mixes)
    row = lambda i: (i, 0)
    const2 = lambda i: (0, 0)
    in_specs = [pl.BlockSpec((FFN_TM, d), row)]
    in_specs += [pl.BlockSpec((FFN_TM, m.shape[1]), row) for m in mixes]
    once = pl.Buffered(1)
    row0 = 0
    for m in mixes:
        in_specs.append(pl.BlockSpec((m.shape[1], d), lambda i, r=row0 // m.shape[1]: (r, 0),
                                     pipeline_mode=once))
        row0 += m.shape[1]
    in_specs += [
        pl.BlockSpec(g3.shape, const2),
        pl.BlockSpec((None, None) + wg.shape[2:], lambda i: which + (0, 0), pipeline_mode=once),
        pl.BlockSpec((None, None) + wu.shape[2:], lambda i: which + (0, 0), pipeline_mode=once),
        pl.BlockSpec((None, None) + wd.shape[2:], lambda i: which + (0, 0), pipeline_mode=once),
    ]
    return pl.pallas_call(
        functools.partial(_ffn_kernel, n_mix=n_mix, n_chunks=nc),
        grid=(n // FFN_TM,),
        in_specs=in_specs,
        out_specs=pl.BlockSpec((FFN_TM, d), row),
        out_shape=jax.ShapeDtypeStruct((n, d), F32),
        compiler_params=_params("parallel"),
        name="ffn_mix" if n_mix else "ffn",
    )(x2d, *mixes, *([w_out] * n_mix), g3, wg, wu, wd)


def _rope(p, cos_t, sin_t, half):
    lane = lax.broadcasted_iota(jnp.int32, p.shape, 1)
    first = (lane & (2 * half - 1)) < half
    partner = jnp.where(first, pltpu.roll(p, LANES - half, 1), pltpu.roll(p, half, 1))
    return p * cos_t + partner * sin_t


def _proj_even_kernel(x_ref, g_ref, w_ref, tab_ref, qk_ref, avt_ref, bvt_ref, idx_ref, iwt_ref):
    h = _rms(x_ref[...], g_ref[0:1, :]).astype(BF16)
    c64, s64, c32, s32 = tab_ref[0], tab_ref[1], tab_ref[2], tab_ref[3]
    q_scale = HEAD_DIM ** -0.5 * LOG2E
    out_col = 0
    for c in (0, 1, 3, 4):
        p = _dot(h, w_ref[:, c * PROJ_CC:(c + 1) * PROJ_CC])
        for j in range(PROJ_CC // LANES):
            pj = _rope(p[:, j * LANES:(j + 1) * LANES], c64, s64, HEAD_DIM // 2)
            if c in (0, 3):
                pj = pj * q_scale
            qk_ref[:, out_col:out_col + LANES] = pj.astype(BF16)
            out_col += LANES
    avt_ref[...] = _dot(h, w_ref[:, 2 * PROJ_CC:3 * PROJ_CC]).T.astype(BF16)
    bvt_ref[0] = _dot(h, w_ref[:, 5 * PROJ_CC:6 * PROJ_CC]).T.astype(BF16)
    p = _dot(h, w_ref[:, 6 * PROJ_CC:7 * PROJ_CC])
    for j in range(3):
        pj = _rope(p[:, j * LANES:(j + 1) * LANES], c32, s32, IDX_DIM // 2)
        idx_ref[:, j * LANES:(j + 1) * LANES] = pj.astype(BF16)
    iw = p[:, 3 * LANES:4 * LANES] * (IDX_HEADS ** -0.5 * IDX_DIM ** -0.5)
    iwt_ref[...] = iw.T[0:IDX_HEADS, :]


def _proj_even_call(x2d, g, w, tabs, seq):
    n, d = x2d.shape
    steps_per_seq = seq // PROJ_TM
    return pl.pallas_call(
        _proj_even_kernel,
        grid=(n // PROJ_TM,),
        in_specs=[
            pl.BlockSpec((PROJ_TM, d), lambda i: (i, 0)),
            pl.BlockSpec(g.shape, lambda i: (0, 0)),
            pl.BlockSpec(w.shape, lambda i: (0, 0)),
            pl.BlockSpec((4, PROJ_TM, LANES), lambda i: (0, i % steps_per_seq, 0)),
        ],
        out_specs=[
            pl.BlockSpec((PROJ_TM, 4 * PROJ_CC), lambda i: (i, 0)),
            pl.BlockSpec((PROJ_CC, PROJ_TM), lambda i: (0, i)),
            pl.BlockSpec((1, PROJ_CC, PROJ_TM), lambda i: (i, 0, 0)),
            pl.BlockSpec((PROJ_TM, 3 * LANES), lambda i: (i, 0)),
            pl.BlockSpec((IDX_HEADS, PROJ_TM), lambda i: (0, i)),
        ],
        out_shape=[
            jax.ShapeDtypeStruct((n, 4 * PROJ_CC), BF16),
            jax.ShapeDtypeStruct((PROJ_CC, n), BF16),
            jax.ShapeDtypeStruct((n // PROJ_TM, PROJ_CC, PROJ_TM), BF16),
            jax.ShapeDtypeStruct((n, 3 * LANES), BF16),
            jax.ShapeDtypeStruct((IDX_HEADS, n), F32),
        ],
        compiler_params=_params("parallel"),
        name="proj_even",
    )(x2d, g, w, tabs)


def _proj_odd_kernel(x_ref, g_ref, w_ref, qk_ref, vt_ref):
    h = _rms(x_ref[...], g_ref[0:1, :]).astype(BF16)
    q_scale = HEAD_DIM ** -0.5 * LOG2E
    for c in range(2 * D_MODEL // PROJ_CC):
        p = _dot(h, w_ref[:, c * PROJ_CC:(c + 1) * PROJ_CC])
        if c * PROJ_CC < D_MODEL:
            p = p * q_scale
        qk_ref[:, c * PROJ_CC:(c + 1) * PROJ_CC] = p.astype(BF16)
    for c in range(D_MODEL // PROJ_CC):
        col = 2 * D_MODEL + c * PROJ_CC
        vt = _dot(h, w_ref[:, col:col + PROJ_CC]).T.astype(BF16)
        for j in range(vt_ref.shape[0]):
            vt_ref[j, c * PROJ_CC:(c + 1) * PROJ_CC, :] = vt[:, j * BAND_TQ:(j + 1) * BAND_TQ]


def _proj_odd_call(x2d, g, w):
    n, d = x2d.shape
    return pl.pallas_call(
        _proj_odd_kernel,
        grid=(n // PROJ_TM,),
        in_specs=[
            pl.BlockSpec((PROJ_TM, d), lambda i: (i, 0)),
            pl.BlockSpec(g.shape, lambda i: (0, 0)),
            pl.BlockSpec(w.shape, lambda i: (0, 0)),
        ],
        out_specs=[
            pl.BlockSpec((PROJ_TM, 2 * d), lambda i: (i, 0)),
            pl.BlockSpec((PROJ_TM // BAND_TQ, d, BAND_TQ), lambda i: (i, 0, 0)),
        ],
        out_shape=[
            jax.ShapeDtypeStruct((n, 2 * d), BF16),
            jax.ShapeDtypeStruct((n // BAND_TQ, d, BAND_TQ), BF16),
        ],
        compiler_params=_params("parallel"),
        name="proj_odd",
    )(x2d, g, w)


def _diff_attn_tile(c, tq, q_ref, k_ref, vt_ref, lam, sub_ref, o_ref, lambda_init):
    n_heads = q_ref.shape[1] // LANES
    left = c * tq
    lane = lax.broadcasted_iota(jnp.int32, (tq, LANES), 1)
    key_chunk = lax.broadcasted_iota(jnp.int32, (tq, tq), 0) >> 6
    qry_chunk = lax.broadcasted_iota(jnp.int32, (tq, tq), 1) >> 6
    diag_mask = key_chunk <= qry_chunk
    raw = []
    for h in range(n_heads):
        sl = slice(h * LANES, (h + 1) * LANES)
        q = q_ref[left:left + tq, sl]
        for keep in (lane < HEAD_DIM, lane >= HEAD_DIM):
            qc = jnp.where(keep, q, jnp.zeros_like(q))
            parts = [jnp.where(diag_mask, _dot_nt(k_ref[left:left + tq, sl], qc), NEG)]
            if c:
                parts.append(_dot_nt(k_ref[0:left, sl], qc))
            raw.append(parts)
    comps = []
    for parts in raw:
        m = functools.reduce(jnp.maximum, [jnp.max(s, axis=0, keepdims=True) for s in parts])
        parts = [jnp.exp2(s - m) for s in parts]
        l = functools.reduce(jnp.add, [jnp.sum(p, axis=0, keepdims=True) for p in parts])
        comps.append((parts, l))
    for h in range(n_heads):
        rows = slice(h * A_VDIM, (h + 1) * A_VDIM)
        outs = []
        for parts, l in (comps[2 * h], comps[2 * h + 1]):
            acc = _dot(vt_ref[rows, left:left + tq], parts[0].astype(BF16))
            if c:
                acc = acc + _dot(vt_ref[rows, 0:left], parts[1].astype(BF16))
            outs.append(acc / l)
        o_t = outs[0] - lam * outs[1]
        ms = jnp.mean(o_t * o_t, axis=0, keepdims=True)
        o_t = o_t * lax.rsqrt(ms + EPS) * sub_ref[...] * (1.0 - lambda_init)
        o_ref[left:left + tq, h * LANES:(h + 1) * LANES] = o_t.T.astype(o_ref.dtype)


def _diff_attn_kernel(q_ref, k_ref, vt_ref, lam_ref, sub_ref, o_ref, *, lambda_init, tq):
    lp = lam_ref[...]
    lam = (jnp.exp(jnp.sum(lp[0:1] * lp[1:2], axis=-1, keepdims=True))
           - jnp.exp(jnp.sum(lp[2:3] * lp[3:4], axis=-1, keepdims=True)) + lambda_init)
    for c in range(q_ref.shape[0] // tq):
        _diff_attn_tile(c, tq, q_ref, k_ref, vt_ref, lam, sub_ref, o_ref, lambda_init)


def _diff_attn_call(qk, avt, lam_p, subln, batch, seq, lambda_init):
    n = qk.shape[0]
    n_groups = A_HEADS // DIFF_HEADS
    width = DIFF_HEADS * A_VDIM
    sub_t = jnp.broadcast_to(subln.astype(F32)[:, None], (A_VDIM, DIFF_TQ))
    return pl.pallas_call(
        functools.partial(_diff_attn_kernel, lambda_init=lambda_init, tq=DIFF_TQ),
        grid=(batch, n_groups),
        in_specs=[
            pl.BlockSpec((seq, width), lambda b, g: (b, g)),
            pl.BlockSpec((seq, width), lambda b, g: (b, n_groups + g)),
            pl.BlockSpec((width, seq), lambda b, g: (g, b)),
            pl.BlockSpec(lam_p.shape, lambda b, g: (0, 0)),
            pl.BlockSpec(sub_t.shape, lambda b, g: (0, 0)),
        ],
        out_specs=pl.BlockSpec((seq, width), lambda b, g: (b, g)),
        out_shape=jax.ShapeDtypeStruct((n, A_HEADS * A_VDIM), BF16),
        compiler_params=_params("parallel", "parallel"),
        name="diff_attn",
    )(qk, qk, avt, lam_p, sub_t)


def _dsa_kernel(iq_ref, ik_ref, iwt_ref, q_ref, k_ref, vt_ref, o_ref,
                qh_ref, qm_ref, score_ref, coarse_ref, acc_ref, *, topk):
    i = pl.program_id(1)
    tq = q_ref.shape[0]
    kb_size = score_ref.shape[1]
    n_kb = i + 1
    kf = float(topk)

    lane = lax.broadcasted_iota(jnp.int32, (tq, LANES), 1)
    idx_per_vreg = LANES // IDX_DIM
    for h in range(IDX_HEADS):
        x = iq_ref[:, (h // idx_per_vreg) * LANES:(h // idx_per_vreg + 1) * LANES]
        qh_ref[h] = jnp.where((lane >> 5) == (h % idx_per_vreg), x, jnp.zeros_like(x))
    for h in range(B_HEADS):
        x = q_ref[:, (h // 2) * LANES:(h // 2 + 1) * LANES]
        qm_ref[h] = jnp.where((lane >> 6) == (h % 2), x, jnp.zeros_like(x))

    q_chunk = (i * tq + lax.broadcasted_iota(jnp.int32, (kb_size, tq), 1)) >> 6

    def visible(kb):
        k_chunk = (kb * kb_size + lax.broadcasted_iota(jnp.int32, (kb_size, tq), 0)) >> 6
        return k_chunk <= q_chunk

    def score_block(kb, carry):
        start = pl.multiple_of(kb * kb_size, kb_size)
        ikb = ik_ref[pl.ds(start, kb_size), :]
        dots = [_dot_nt(ikb, qh_ref[h]) for h in range(IDX_HEADS)]
        score = jnp.maximum(dots[0], 0.0) * iwt_ref[0:1, :]
        for h in range(1, IDX_HEADS):
            score = score + jnp.maximum(dots[h], 0.0) * iwt_ref[h:h + 1, :]
        score = jnp.where(visible(kb), score + 0.0, -jnp.inf)
        score_ref[kb] = score
        coarse_ref[kb] = score.astype(BF16)
        return carry

    lax.fori_loop(0, n_kb, score_block, 0)

    def count(pred):
        def block(kb, acc):
            for r in range(kb_size // 8):
                x = score_ref[kb, r * 8:(r + 1) * 8, :]
                acc = acc + jnp.where(pred(x), 1.0, 0.0)
            return acc
        acc = lax.fori_loop(0, n_kb, block, jnp.zeros((8, tq), F32))
        return jnp.sum(acc, axis=0, keepdims=True)

    pack = 16
    one_b = jnp.ones((pack, tq), BF16)
    zero_b = jnp.zeros((pack, tq), BF16)

    def count_coarse(pred):
        def block(kb, acc):
            for r in range(kb_size // pack):
                x = coarse_ref[kb, r * pack:(r + 1) * pack, :]
                acc = acc + jnp.where(pred(x), one_b, zero_b)
            return acc
        acc = lax.fori_loop(0, n_kb, block, zero_b)
        return jnp.sum(acc.astype(F32), axis=0, keepdims=True)

    def key_to_float(key):
        return pltpu.bitcast(key ^ ((key >> 31) & 0x7FFFFFFF), F32)

    def key16_to_key(k16):
        return lax.shift_left(k16, 16) | ((k16 >> 31) & 0xFFFF)

    neg_inf_key16 = -32641
    neg_inf_key = -2139095041
    top_key16 = 32767

    def coarse_step(it, t):
        cand = t + lax.shift_left(jnp.int32(1), 15 - it)
        c_f = key_to_float(key16_to_key(jnp.minimum(cand, top_key16)))
        c_b = jnp.broadcast_to(c_f, (pack, tq)).astype(BF16)
        cnt = count_coarse(lambda x: x >= c_b)
        return jnp.where((cand <= top_key16) & (cnt >= kf), cand, t)

    t16 = lax.fori_loop(0, 16, coarse_step, jnp.full((1, tq), neg_inf_key16, jnp.int32))

    def fine_step(it, t):
        cand = t + lax.shift_left(jnp.int32(1), 16 - it)
        c_f = key_to_float(cand)
        cnt = count(lambda x: x >= c_f)
        return jnp.where((cand > t) & (cnt >= kf), cand, t)

    thr_key = lax.fori_loop(0, 17, fine_step, jnp.maximum(key16_to_key(t16 - 1), neg_inf_key))
    thr = key_to_float(thr_key)
    need = kf - count(lambda x: x > thr)

    r_i = lax.broadcasted_iota(jnp.int32, (LANES, LANES), 0)
    c_i = lax.broadcasted_iota(jnp.int32, (LANES, LANES), 1)
    strict_lower = jnp.where(c_i < r_i, 1.0, 0.0).astype(BF16)

    def select_bias(kb, offset):
        vis = visible(kb)
        rows = []
        for j in range(kb_size // LANES):
            sl = slice(j * LANES, (j + 1) * LANES)
            sj = score_ref[kb, sl, :]
            eq = sj == thr
            eq_f = jnp.where(eq, 1.0, 0.0)
            before = _dot(strict_lower, eq_f.astype(BF16)) + offset
            sel = ((sj > thr) | (eq & (before < need))) & vis[sl]
            rows.append(jnp.where(sel, 0.0, NEG))
            offset = offset + jnp.sum(eq_f, axis=0, keepdims=True)
        return jnp.concatenate(rows, axis=0), offset

    acc_ref[...] = jnp.zeros(acc_ref.shape, F32)

    def attn_block(kb, carry):
        ms, ls, offset = carry
        start = pl.multiple_of(kb * kb_size, kb_size)
        bias, offset = select_bias(kb, offset)
        new_m, new_l = [], []
        scores = []
        for h in range(B_HEADS):
            k2 = k_ref[pl.ds(start, kb_size), (h // 2) * LANES:(h // 2 + 1) * LANES]
            s = _dot_nt(k2, qm_ref[h]) + bias
            scores.append(s)
            new_m.append(jnp.maximum(ms[h], jnp.max(s, axis=0, keepdims=True)))
        for h in range(B_HEADS):
            alpha = jnp.exp2(ms[h] - new_m[h])
            p = jnp.exp2(scores[h] - new_m[h])
            new_l.append(alpha * ls[h] + jnp.sum(p, axis=0, keepdims=True))
            pv = _dot(vt_ref[kb, h * HEAD_DIM:(h + 1) * HEAD_DIM, :], p.astype(BF16))
            acc_ref[h] = alpha * acc_ref[h] + pv
        return tuple(new_m), tuple(new_l), offset

    m0 = tuple(jnp.full((1, tq), NEG, F32) for _ in range(B_HEADS))
    l0 = tuple(jnp.zeros((1, tq), F32) for _ in range(B_HEADS))
    _, ls, _ = lax.fori_loop(0, n_kb, attn_block, (m0, l0, jnp.zeros((1, tq), F32)))
    o_t = jnp.concatenate([acc_ref[h] / ls[h] for h in range(B_HEADS)], axis=0)
    o_ref[...] = o_t.T.astype(o_ref.dtype)


def _dsa_call(qk, vt, idx, iwt, batch, seq, topk):
    n = qk.shape[0]
    nq = seq // DSA_TQ
    width = B_HEADS * HEAD_DIM
    assert vt.shape[2] == DSA_TQ, "key blocks must match the projection's token tile"
    return pl.pallas_call(
        functools.partial(_dsa_kernel, topk=topk),
        grid=(batch, nq),
        in_specs=[
            pl.BlockSpec((DSA_TQ, 2 * LANES), lambda b, i: (b * nq + i, 0)),
            pl.BlockSpec((seq, LANES), lambda b, i: (b, 2)),
            pl.BlockSpec((IDX_HEADS, DSA_TQ), lambda b, i: (0, b * nq + i)),
            pl.BlockSpec((DSA_TQ, width), lambda b, i: (b * nq + i, 2)),
            pl.BlockSpec((seq, width), lambda b, i: (b, 3)),
            pl.BlockSpec((nq, width, DSA_TQ), lambda b, i: (b, 0, 0)),
        ],
        out_specs=pl.BlockSpec((DSA_TQ, width), lambda b, i: (b * nq + i, 0)),
        out_shape=jax.ShapeDtypeStruct((n, width), BF16),
        scratch_shapes=[
            pltpu.VMEM((IDX_HEADS, DSA_TQ, LANES), BF16),
            pltpu.VMEM((B_HEADS, DSA_TQ, LANES), BF16),
            pltpu.VMEM((nq, DSA_TQ, DSA_TQ), F32),
            pltpu.VMEM((nq, DSA_TQ, DSA_TQ), BF16),
            pltpu.VMEM((B_HEADS, HEAD_DIM, DSA_TQ), F32),
        ],
        compiler_params=_params("parallel", "arbitrary"),
        name="dsa",
    )(idx, idx, iwt, qk, qk, vt)


def _band_attn_kernel(q_ref, k_ref, vt_ref, tbl_ref, o_ref):
    n_heads = tbl_ref.shape[0]
    tq = tbl_ref.shape[2]
    n_blk = tbl_ref.shape[1] // tq
    n_tiles = q_ref.shape[0] // tq
    lane = lax.broadcasted_iota(jnp.int32, (tq, LANES), 1)
    for c in range(n_tiles):
        blocks = [(d, c - (n_blk - 1) + d) for d in range(n_blk) if c - (n_blk - 1) + d >= 0]
        scores, ms = [], []
        for h in range(n_heads):
            sl = slice((h // 2) * LANES, (h // 2 + 1) * LANES)
            q2 = q_ref[c * tq:(c + 1) * tq, sl]
            qm = jnp.where((lane >> 6) == (h % 2), q2, jnp.zeros_like(q2))
            parts = [_dot_nt(k_ref[kb * tq:(kb + 1) * tq, sl], qm) + tbl_ref[h, d * tq:(d + 1) * tq, :]
                     for d, kb in blocks]
            scores.append(parts)
            ms.append(functools.reduce(jnp.maximum, [jnp.max(s, axis=0, keepdims=True) for s in parts]))
        outs = []
        for h in range(n_heads):
            ps = [jnp.exp2(s - ms[h]) for s in scores[h]]
            l = functools.reduce(jnp.add, [jnp.sum(p, axis=0, keepdims=True) for p in ps])
            rows = slice(h * HEAD_DIM, (h + 1) * HEAD_DIM)
            pv = functools.reduce(jnp.add, [_dot(vt_ref[kb, rows, :], p.astype(BF16))
                                            for (d, kb), p in zip(blocks, ps)])
            outs.append(pv / l)
        o_ref[c * tq:(c + 1) * tq, :] = jnp.concatenate(outs, axis=0).T.astype(o_ref.dtype)


def _band_attn_call(qk, vt, tbl, batch, seq):
    n = qk.shape[0]
    nq = seq // BAND_TQ
    n_groups = C_HEADS // BAND_HEADS
    width = BAND_HEADS * HEAD_DIM
    return pl.pallas_call(
        _band_attn_kernel,
        grid=(n_groups, batch),
        in_specs=[
            pl.BlockSpec((seq, width), lambda g, b: (b, g)),
            pl.BlockSpec((seq, width), lambda g, b: (b, n_groups + g)),
            pl.BlockSpec((nq, width, BAND_TQ), lambda g, b: (b, g, 0)),
            pl.BlockSpec((BAND_HEADS,) + tbl.shape[1:], lambda g, b: (g, 0, 0)),
        ],
        out_specs=pl.BlockSpec((seq, width), lambda g, b: (b, g)),
        out_shape=jax.ShapeDtypeStruct((n, D_MODEL), BF16),
        compiler_params=_params("parallel", "parallel"),
        name="band_attn",
    )(qk, qk, vt, tbl)


def _rope_tables(seq):
    def one(dim):
        inv = ROPE_THETA ** (-jnp.arange(0, dim, 2, dtype=F32) / dim)
        ang = jnp.arange(seq, dtype=F32)[:, None] * inv[None, :]
        cos, sin = jnp.cos(ang), jnp.sin(ang)
        reps = LANES // dim
        return (jnp.tile(jnp.concatenate([cos, cos], axis=-1), (1, reps)),
                jnp.tile(jnp.concatenate([-sin, sin], axis=-1), (1, reps)))
    c64, s64 = one(HEAD_DIM)
    c32, s32 = one(IDX_DIM)
    return jnp.stack([c64, s64, c32, s32])


def _even_w_in_layout(w):
    d = w.shape[0]
    n_qkv = 6 * PROJ_CC
    iq = w[:, n_qkv:n_qkv + IDX_HEADS * IDX_DIM]
    ik = w[:, n_qkv + IDX_HEADS * IDX_DIM:n_qkv + IDX_HEADS * IDX_DIM + IDX_DIM]
    iw = w[:, n_qkv + IDX_HEADS * IDX_DIM + IDX_DIM:]
    return jnp.concatenate(
        [w[:, :n_qkv], iq, jnp.tile(ik, (1, LANES // IDX_DIM)),
         iw, jnp.zeros((d, LANES - IDX_HEADS), w.dtype)], axis=1).astype(BF16)


def _band_table_kernel(g_ref, o_ref):
    window, tq = o_ref.shape[1], o_ref.shape[2]
    span = g_ref.shape[2]
    gb = jnp.broadcast_to(g_ref[0], (window, span))
    tbl = pltpu.roll(gb, span - window, 1, stride=1, stride_axis=0)[:, :tq]
    kc = lax.broadcasted_iota(jnp.int32, (window, tq), 0) >> 6
    qc = (window - tq + lax.broadcasted_iota(jnp.int32, (window, tq), 1)) >> 6
    in_band = (kc <= qc) & (kc >= qc - C_LEFT_CHUNKS)
    o_ref[0] = jnp.where(in_band, tbl * LOG2E, NEG)


def _band_bias_table(rel_bias):
    h = rel_bias.shape[0]
    left = C_LEFT_CHUNKS * CHUNK
    window = left + BAND_TQ
    span = BAND_TQ + window
    n_near = window - left + 1 - CHUNK
    n_far = span - n_near - REL_TABLE
    rel = rel_bias.astype(F32)
    g = jnp.concatenate([jnp.broadcast_to(rel[:, :1], (h, n_near)), rel,
                         jnp.broadcast_to(rel[:, REL_TABLE - 1:], (h, n_far))], axis=1)
    return pl.pallas_call(
        _band_table_kernel,
        grid=(h,),
        in_specs=[pl.BlockSpec((1, 1, span), lambda i: (i, 0, 0))],
        out_specs=pl.BlockSpec((1, window, BAND_TQ), lambda i: (i, 0, 0)),
        out_shape=jax.ShapeDtypeStruct((h, window, BAND_TQ), F32),
        compiler_params=_params("parallel"),
        name="band_table",
    )(g[:, None, :])


def kernel(x, norm_g, ffn_wg, ffn_wu, ffn_wd, even_w_in, even_w_out, even_lambda, even_subln,
           odd_w_in, odd_w_out, odd_rel_bias):
    batch, seq, d = x.shape
    depth = norm_g.shape[0]
    topk = min(TOPK_MAX, seq // 4)
    x2d = x.reshape(batch * seq, d)
    tabs = _rope_tables(seq)
    ffn_w = (ffn_wg.astype(BF16), ffn_wu.astype(BF16), ffn_wd.astype(BF16))
    pending = ((), None)
    g_mix = norm_g[0, 3]
    for l in range(depth):
        g = norm_g[l]
        x2d = _ffn_call(x2d, pending[0], pending[1], jnp.stack([g_mix, g[0], g[1]]), *ffn_w, (l, 0))
        if l % 2 == 0:
            e = l // 2
            lambda_init = 0.8 - 0.6 * math.exp(-0.3 * l)
            qk, avt, bvt, idx, iwt = _proj_even_call(x2d, g[2:3], _even_w_in_layout(even_w_in[e]), tabs, seq)
            o_a = _diff_attn_call(qk, avt, even_lambda[e], even_subln[e], batch, seq, lambda_init)
            o_b = _dsa_call(qk, bvt, idx, iwt, batch, seq, topk)
            pending = ((o_a, o_b), even_w_out[e].astype(BF16))
        else:
            o = l // 2
            qk, vt = _proj_odd_call(x2d, g[2:3], odd_w_in[o].astype(BF16))
            o_c = _band_attn_call(qk, vt, _band_bias_table(odd_rel_bias[o]), batch, seq)
            pending = ((o_c,), odd_w_out[o].astype(BF16))
        g_mix = g[3]
        x2d = _ffn_call(x2d, pending[0], pending[1], jnp.stack([g_mix, g[4], g[5]]), *ffn_w, (l, 1))
        pending = ((), None)
    return x2d.reshape(batch, seq, d)
```
